```python
import math
import jax, jax.numpy as jnp
from jax import lax
import numpy as np

D_MODEL = 2048
BATCH = 4
SEQ = 2048
DEPTH = 1

MIX_WIDTH = D_MODEL
ATTN_HEADS = 8
HEAD_DIM = 128
ATTN_WIDTH = ATTN_HEADS * HEAD_DIM
LRU_HEADS = 8
LRU_WIDTH = MIX_WIDTH - ATTN_WIDTH
LRU_BLOCK = LRU_WIDTH // LRU_HEADS
CONV_WIDTH = 4
LRU_C = 8.0
D_FF = 4 * D_MODEL
Q_BLOCK = 128
N_MOD = 6
IN_COLS = 3 * ATTN_WIDTH + 2 * LRU_WIDTH
EPS = 1e-6

kernel_name = "hymba_stickbreak_rglru_sqrelu_adaln"


def rmsnorm(x, g):
    x32 = x.astype(jnp.float32)
    y = x32 * lax.rsqrt(jnp.mean(x32 * x32, axis=-1, keepdims=True) + EPS)
    return (y * g.astype(jnp.float32)).astype(x.dtype)


def stick_breaking_attention(q, k, v):
    B, S, H, Dh = q.shape
    scale = Dh ** -0.5
    outs = []
    for i in range(S // Q_BLOCK):
        q0 = i * Q_BLOCK
        kend = q0 + Q_BLOCK
        qb = q[:, q0:kend].astype(jnp.float32)
        kb = k[:, :kend].astype(jnp.float32)
        vb = v[:, :kend].astype(jnp.float32)
        z = jnp.einsum('bqhd,bkhd->bhqk', qb, kb) * scale
        t_idx = q0 + jnp.arange(Q_BLOCK)[:, None]
        s_idx = jnp.arange(kend)[None, :]
        causal = s_idx < t_idx
        log_beta = jax.nn.log_sigmoid(z)
        log_stay = jnp.where(causal, jax.nn.log_sigmoid(-z), 0.0)
        rev = lax.cumsum(log_stay, axis=3, reverse=True)
        after = jnp.concatenate([rev[..., 1:], jnp.zeros_like(rev[..., :1])], axis=-1)
        w = jnp.where(causal, jnp.exp(log_beta + after), 0.0)
        outs.append(jnp.einsum('bhqk,bkhd->bqhd', w, vb))
    return jnp.concatenate(outs, axis=1).astype(q.dtype)


def causal_depthwise_conv(x, w, b):
    C = x.shape[-1]
    y = lax.conv_general_dilated(
        x.astype(jnp.float32), w.astype(jnp.float32)[:, None, :],
        window_strides=(1,), padding=[(CONV_WIDTH - 1, 0)],
        dimension_numbers=('NWC', 'WIO', 'NWC'), feature_group_count=C)
    return y + b.astype(jnp.float32)


def rg_lru(x, w_a, b_a, w_x, b_x, lam):
    B, S, _ = x.shape
    xh = x.reshape(B, S, LRU_HEADS, LRU_BLOCK)
    r = jax.nn.sigmoid(jnp.einsum('bshc,hcd->bshd', xh, w_a.astype(jnp.float32)).reshape(B, S, LRU_WIDTH) + b_a)
    i = jax.nn.sigmoid(jnp.einsum('bshc,hcd->bshd', xh, w_x.astype(jnp.float32)).reshape(B, S, LRU_WIDTH) + b_x)
    log_a = -LRU_C * r * jax.nn.softplus(-lam.astype(jnp.float32))
    a = jnp.exp(log_a)
    u = jnp.sqrt(-jnp.expm1(2.0 * log_a)) * (i * x)

    def combine(left, right):
        a1, b1 = left
        a2, b2 = right
        return a1 * a2, a2 * b1 + b2

    _, h = lax.associative_scan(combine, (a, u), axis=1)
    return h


def setup_inputs(seed: int = 0) -> dict:
    key = jax.random.key(seed)
    ks = jax.random.split(key, 24)
    f32 = jnp.float32
    nrm = lambda k, shape, s: jax.random.normal(k, shape, f32) * s
    u = jax.random.uniform(ks[11], (DEPTH, LRU_WIDTH), f32, 0.9, 0.999)
    a_base = u ** (1.0 / LRU_C)
    lru_lambda = jnp.log(a_base) - jnp.log1p(-a_base)
    return {
        "x": nrm(ks[0], (BATCH, SEQ, D_MODEL), 1.0),
        "c": nrm(ks[1], (BATCH, D_MODEL), 1.0),
        "w_ada": nrm(ks[2], (DEPTH, D_MODEL, N_MOD * D_MODEL), 0.5 * D_MODEL ** -0.5),
        "b_ada": nrm(ks[3], (DEPTH, N_MOD * D_MODEL), 0.02),
        "g_norm_mix": 1.0 + nrm(ks[4], (DEPTH, D_MODEL), 0.02),
        "w_in": nrm(ks[5], (DEPTH, D_MODEL, IN_COLS), D_MODEL ** -0.5),
        "w_conv": nrm(ks[6], (DEPTH, CONV_WIDTH, LRU_WIDTH), CONV_WIDTH ** -0.5),
        "b_conv": nrm(ks[7], (DEPTH, LRU_WIDTH), 0.02),
        "w_rg_a": nrm(ks[8], (DEPTH, LRU_HEADS, LRU_BLOCK, LRU_BLOCK), LRU_BLOCK ** -0.5),
        "b_rg_a": nrm(ks[9], (DEPTH, LRU_WIDTH), 0.02),
        "w_rg_x": nrm(ks[10], (DEPTH, LRU_HEADS, LRU_BLOCK, LRU_BLOCK), LRU_BLOCK ** -0.5),
        "b_rg_x": nrm(ks[12], (DEPTH, LRU_WIDTH), 0.02),
        "lru_lambda": lru_lambda,
        "g_attn_out": 1.0 + nrm(ks[13], (DEPTH, ATTN_WIDTH), 0.02),
        "g_lru_out": 1.0 + nrm(ks[14], (DEPTH, LRU_WIDTH), 0.02),
        "w_out": nrm(ks[15], (DEPTH, MIX_WIDTH, D_MODEL), MIX_WIDTH ** -0.5),
        "g_norm_mlp": 1.0 + nrm(ks[16], (DEPTH, D_MODEL), 0.02),
        "w_mlp_in": nrm(ks[17], (DEPTH, D_MODEL, D_FF), D_MODEL ** -0.5),
        "w_mlp_out": nrm(ks[18], (DEPTH, D_FF, D_MODEL), D_FF ** -0.5),
        "g_norm_final": 1.0 + nrm(ks[19], (D_MODEL,), 0.02),
    }


def reference(x, c, w_ada, b_ada, g_norm_mix, w_in, w_conv, b_conv, w_rg_a, b_rg_a,
              w_rg_x, b_rg_x, lru_lambda, g_attn_out, g_lru_out, w_out, g_norm_mlp,
              w_mlp_in, w_mlp_out, g_norm_final):
    B, S, D = x.shape
    c_act = jax.nn.silu(c.astype(jnp.float32))
    for l in range(DEPTH):
        mod = (c_act @ w_ada[l].astype(jnp.float32) + b_ada[l]).reshape(B, N_MOD, D)
        sh1, sc1, gt1, sh2, sc2, gt2 = [mod[:, j][:, None, :] for j in range(N_MOD)]

        h = rmsnorm(x, g_norm_mix[l]).astype(jnp.float32) * (1.0 + sc1) + sh1
        proj = jnp.einsum('bsd,de->bse', h, w_in[l].astype(jnp.float32))
        q, k, v, xr, xg = jnp.split(
            proj, np.cumsum([ATTN_WIDTH, ATTN_WIDTH, ATTN_WIDTH, LRU_WIDTH]).tolist(), axis=-1)

        to_heads = lambda t: t.reshape(B, S, ATTN_HEADS, HEAD_DIM)
        o_attn = stick_breaking_attention(to_heads(q), to_heads(k), to_heads(v)).reshape(B, S, ATTN_WIDTH)

        xr = causal_depthwise_conv(xr, w_conv[l], b_conv[l])
        hr = rg_lru(xr, w_rg_a[l], b_rg_a[l], w_rg_x[l], b_rg_x[l], lru_lambda[l])
        o_lru = hr * jax.nn.gelu(xg, approximate=True)

        mixed = jnp.concatenate([rmsnorm(o_attn, g_attn_out[l]), rmsnorm(o_lru, g_lru_out[l])], axis=-1)
        y = jnp.einsum('bse,ed->bsd', mixed, w_out[l].astype(jnp.float32))
        x = (x.astype(jnp.float32) + gt1 * y).astype(x.dtype)

        h = rmsnorm(x, g_norm_mlp[l]).astype(jnp.float32) * (1.0 + sc2) + sh2
        hid = jnp.square(jax.nn.relu(jnp.einsum('bsd,df->bsf', h, w_mlp_in[l].astype(jnp.float32))))
        y = jnp.einsum('bsf,fd->bsd', hid, w_mlp_out[l].astype(jnp.float32))
        x = (x.astype(jnp.float32) + gt2 * y).astype(x.dtype)

    return rmsnorm(x, g_norm_final)
```

```python
import functools

import jax
import jax.numpy as jnp
from jax import lax
from jax.experimental import pallas as pl
from jax.experimental.pallas import tpu as pltpu

F32 = jnp.float32
BF16 = jnp.bfloat16

ATTN_HEADS = 8
HEAD_DIM = 128
LRU_HEADS = 8
LRU_BLOCK = 128
CONV_WIDTH = 4
LRU_C = 8.0
N_MOD = 6
EPS = 1e-6

EXP_UNDERFLOW = -104.0

VMEM_LIMIT = 56 * 1024 * 1024


def _softplus(x):
    return jnp.maximum(x, 0.0) + jnp.log1p(jnp.exp(-jnp.abs(x)))


def _rms_scale(x):
    return lax.rsqrt(jnp.mean(x * x, axis=-1, keepdims=True) + EPS)


def _ada_kernel(c_ref, w_ref, b_ref, o_ref):
    c = c_ref[...]
    c_act = c * jax.nn.sigmoid(c)
    o_ref[...] = jnp.dot(c_act.astype(BF16), w_ref[...].astype(BF16),
                         preferred_element_type=F32) + b_ref[...]


def _ada(c_pad, w_ada, b_ada, tn=1024):
    m, d = c_pad.shape
    n = w_ada.shape[1]
    return pl.pallas_call(
        _ada_kernel,
        grid=(n // tn,),
        in_specs=[
            pl.BlockSpec((m, d), lambda j: (0, 0)),
            pl.BlockSpec((d, tn), lambda j: (0, j)),
            pl.BlockSpec((1, tn), lambda j: (0, j)),
        ],
        out_specs=pl.BlockSpec((m, tn), lambda j: (0, j)),
        out_shape=jax.ShapeDtypeStruct((m, n), F32),
        compiler_params=pltpu.CompilerParams(
            dimension_semantics=("arbitrary",), vmem_limit_bytes=VMEM_LIMIT),
        name="ada",
    )(c_pad, w_ada, b_ada)


def _inproj_kernel(x_ref, mod_ref, g_ref, w_ref, qkv_ref, xrg_ref, h_ref):
    j = pl.program_id(1)

    @pl.when(j == 0)
    def _():
        x = x_ref[...]
        y = x * _rms_scale(x) * g_ref[...]
        sh = mod_ref[0, 0:1, :]
        sc = mod_ref[0, 1:2, :]
        h_ref[...] = (y * (1.0 + sc) + sh).astype(BF16)

    res = jnp.dot(h_ref[...], w_ref[...], preferred_element_type=F32)

    @pl.when(j < 3)
    def _():
        qkv_ref[...] = res.astype(BF16)

    @pl.when(j >= 3)
    def _():
        xrg_ref[...] = res


def _inproj(x2, mod, g, w_in, seq, tm=1024, tn=1024):
    t, d = x2.shape
    n = w_in.shape[1]
    nj = n // tn
    return pl.pallas_call(
        _inproj_kernel,
        grid=(t // tm, nj),
        in_specs=[
            pl.BlockSpec((tm, d), lambda i, j: (i, 0)),
            pl.BlockSpec((1, N_MOD, d), lambda i, j: (i * tm // seq, 0, 0)),
            pl.BlockSpec((1, d), lambda i, j: (0, 0)),
            pl.BlockSpec((d, tn), lambda i, j: (0, j)),
        ],
        out_specs=[
            pl.BlockSpec((tm, tn), lambda i, j: (i, jnp.minimum(j, 2))),
            pl.BlockSpec((tm, tn), lambda i, j: (i, jnp.maximum(j - 3, 0))),
        ],
        out_shape=[
            jax.ShapeDtypeStruct((t, 3 * tn), BF16),
            jax.ShapeDtypeStruct((t, 2 * tn), F32),
        ],
        scratch_shapes=[pltpu.VMEM((tm, d), BF16)],
        compiler_params=pltpu.CompilerParams(
            dimension_semantics=("arbitrary", "arbitrary"), vmem_limit_bytes=VMEM_LIMIT),
        name="inproj",
    )(x2, mod, g, w_in)


def _attn_kernel(q_ref, k_ref, v_ref, u_ref, o_ref, *, tq):
    i = pl.program_id(2)
    scale = HEAD_DIM ** -0.5
    q = q_ref[...]
    u = u_ref[...]
    rows = lax.broadcasted_iota(jnp.int32, (tq, tq), 0)
    cols = lax.broadcasted_iota(jnp.int32, (tq, tq), 1)
    causal = cols < rows

    def block(j, c, acc, diagonal):
        start = pl.multiple_of(j * tq, tq)
        k = k_ref[pl.ds(start, tq), :]
        v = v_ref[pl.ds(start, tq), :]
        z = lax.dot_general(q, k, (((1,), (1,)), ((), ())),
                            preferred_element_type=F32) * scale
        t = jnp.log1p(jnp.exp(-jnp.abs(z)))
        log_beta = jnp.minimum(z, 0.0) - t
        log_stay = -jnp.maximum(z, 0.0) - t
        if diagonal:
            log_stay = jnp.where(causal, log_stay, 0.0)
        hi = log_stay.astype(BF16)
        lo = (log_stay - hi.astype(F32)).astype(BF16)
        incl = (jnp.dot(hi, u, preferred_element_type=F32)
                + jnp.dot(lo, u, preferred_element_type=F32))
        after = (incl - log_stay) + c
        w = jnp.exp(log_beta + after)
        if diagonal:
            w = jnp.where(causal, w, 0.0)
        acc = acc + jnp.dot(w.astype(BF16), v, preferred_element_type=F32)
        c = c + incl[:, 0:1]
        return c, acc

    c0 = jnp.zeros((tq, 1), F32)
    acc0 = jnp.zeros((tq, HEAD_DIM), F32)
    c, acc = block(i, c0, acc0, True)

    def cond(state):
        j, alive, _, _ = state
        return jnp.logical_and(j >= 0, alive)

    def body(state):
        j, _, c, acc = state
        c, acc = block(j, c, acc, False)
        return j - 1, jnp.max(c) > EXP_UNDERFLOW, c, acc

    _, _, _, acc = lax.while_loop(cond, body, (i - 1, jnp.max(c) > EXP_UNDERFLOW, c, acc))
    o_ref[...] = acc


def _attention(qkv, batch, seq, tq=256):
    t = qkv.shape[0]
    nq = seq // tq
    r = lax.broadcasted_iota(jnp.int32, (tq, tq), 0)
    s = lax.broadcasted_iota(jnp.int32, (tq, tq), 1)
    u = (r >= s).astype(BF16)
    h = ATTN_HEADS
    return pl.pallas_call(
        functools.partial(_attn_kernel, tq=tq),
        grid=(batch, h, nq),
        in_specs=[
            pl.BlockSpec((tq, HEAD_DIM), lambda b, hh, i: (b * nq + i, hh)),
            pl.BlockSpec((seq, HEAD_DIM), lambda b, hh, i: (b, h + hh)),
            pl.BlockSpec((seq, HEAD_DIM), lambda b, hh, i: (b, 2 * h + hh)),
            pl.BlockSpec((tq, tq), lambda b, hh, i: (0, 0)),
        ],
        out_specs=pl.BlockSpec((tq, HEAD_DIM), lambda b, hh, i: (b * nq + i, hh)),
        out_shape=jax.ShapeDtypeStruct((t, h * HEAD_DIM), F32),
        compiler_params=pltpu.CompilerParams(
            dimension_semantics=("arbitrary", "arbitrary", "arbitrary"),
            vmem_limit_bytes=VMEM_LIMIT),
        name="attn",
    )(qkv, qkv, qkv, u)


def _lru_kernel(xr_ref, xg_ref, wc_ref, bc_ref, wg_ref, ba_ref, bx_ref, lam_ref,
                o_ref, ext_ref, hc_ref, *, ts):
    s = pl.program_id(1)
    hist = 8

    @pl.when(s == 0)
    def _():
        ext_ref[0:hist, :] = jnp.zeros((hist, ext_ref.shape[1]), F32)
        hc_ref[...] = jnp.zeros_like(hc_ref)

    ext_ref[hist:hist + ts, :] = xr_ref[...]
    row = lax.broadcasted_iota(jnp.int32, (ts, LRU_BLOCK), 0)

    for h in range(LRU_HEADS):
        sl = slice(h * LRU_BLOCK, (h + 1) * LRU_BLOCK)
        wc = wc_ref[:, sl]
        xc = bc_ref[:, sl]
        for tap in range(CONV_WIDTH):
            off = hist - (CONV_WIDTH - 1) + tap
            xc = xc + wc[tap:tap + 1, :] * ext_ref[off:off + ts, sl]
        gates = jnp.dot(xc.astype(BF16), wg_ref[h], preferred_element_type=F32)
        r = jax.nn.sigmoid(gates[:, :LRU_BLOCK] + ba_ref[:, sl])
        gi = jax.nn.sigmoid(gates[:, LRU_BLOCK:] + bx_ref[:, sl])
        log_a = (-LRU_C) * r * _softplus(-lam_ref[:, sl])
        a = jnp.exp(log_a)
        th = jnp.tanh(log_a)
        b = jnp.sqrt(-2.0 * th / (1.0 - th)) * (gi * xc)
        d = 1
        while d < ts:
            keep = row >= d
            a_prev = pltpu.roll(a, d, 0)
            b_prev = pltpu.roll(b, d, 0)
            b = jnp.where(keep, a * b_prev, 0.0) + b
            a = jnp.where(keep, a * a_prev, a)
            d *= 2
        hcur = b + a * hc_ref[0:1, sl]
        hc_ref[0:1, sl] = hcur[ts - 1:ts, :]
        o_ref[:, sl] = hcur * jax.nn.gelu(xg_ref[:, sl], approximate=True)

    ext_ref[0:hist, :] = ext_ref[ts:ts + hist, :]


def _lru(xrg, w_conv, b_conv, w_gates, b_a, b_x, lam, batch, seq, ts=256):
    t = xrg.shape[0]
    c = LRU_HEADS * LRU_BLOCK
    ns = seq // ts
    vec = pl.BlockSpec((1, c), lambda b, s: (0, 0))
    return pl.pallas_call(
        functools.partial(_lru_kernel, ts=ts),
        grid=(batch, ns),
        in_specs=[
            pl.BlockSpec((ts, c), lambda b, s: (b * ns + s, 0)),
            pl.BlockSpec((ts, c), lambda b, s: (b * ns + s, 1)),
            pl.BlockSpec((CONV_WIDTH, c), lambda b, s: (0, 0)),
            vec,
            pl.BlockSpec((LRU_HEADS, LRU_BLOCK, 2 * LRU_BLOCK), lambda b, s: (0, 0, 0)),
            vec, vec, vec,
        ],
        out_specs=pl.BlockSpec((ts, c), lambda b, s: (b * ns + s, 0)),
        out_shape=jax.ShapeDtypeStruct((t, c), F32),
        scratch_shapes=[pltpu.VMEM((ts + 8, c), F32), pltpu.VMEM((8, c), F32)],
        compiler_params=pltpu.CompilerParams(
            dimension_semantics=("arbitrary", "arbitrary"), vmem_limit_bytes=VMEM_LIMIT),
        name="lru",
    )(xrg, xrg, w_conv, b_conv, w_gates, b_a, b_x, lam)


def _outproj_kernel(oa_ref, ol_ref, x_ref, mod_ref, ga_ref, gl_ref, w_ref, o_ref):
    oa = oa_ref[...]
    ol = ol_ref[...]
    na = (oa * _rms_scale(oa) * ga_ref[...]).astype(BF16)
    nl = (ol * _rms_scale(ol) * gl_ref[...]).astype(BF16)
    half = oa.shape[1]
    y = (jnp.dot(na, w_ref[0:half, :], preferred_element_type=F32)
         + jnp.dot(nl, w_ref[half:2 * half, :], preferred_element_type=F32))
    o_ref[...] = x_ref[...] + mod_ref[0, 2:3, :] * y


def _outproj(o_attn, o_lru, x2, mod, g_a, g_l, w_out, seq, tm=512):
    t, d = x2.shape
    half = o_attn.shape[1]
    return pl.pallas_call(
        _outproj_kernel,
        grid=(t // tm,),
        in_specs=[
            pl.BlockSpec((tm, half), lambda i: (i, 0)),
            pl.BlockSpec((tm, half), lambda i: (i, 0)),
            pl.BlockSpec((tm, d), lambda i: (i, 0)),
            pl.BlockSpec((1, N_MOD, d), lambda i: (i * tm // seq, 0, 0)),
            pl.BlockSpec((1, half), lambda i: (0, 0)),
            pl.BlockSpec((1, half), lambda i: (0, 0)),
            pl.BlockSpec((2 * half, d), lambda i: (0, 0)),
        ],
        out_specs=pl.BlockSpec((tm, d), lambda i: (i, 0)),
        out_shape=jax.ShapeDtypeStruct((t, d), F32),
        compiler_params=pltpu.CompilerParams(
            dimension_semantics=("arbitrary",), vmem_limit_bytes=VMEM_LIMIT),
        name="outproj",
    )(o_attn, o_lru, x2, mod, g_a, g_l, w_out)


def _mlp_kernel(x_ref, mod_ref, g_ref, w1_ref, w2_ref, gf_ref, o_ref, h_ref, acc_ref):
    j = pl.program_id(1)
    last = pl.num_programs(1) - 1

    @pl.when(j == 0)
    def _():
        x = x_ref[...]
        y = x * _rms_scale(x) * g_ref[...]
        h_ref[...] = (y * (1.0 + mod_ref[0, 4:5, :]) + mod_ref[0, 3:4, :]).astype(BF16)

    hid = jnp.dot(h_ref[...], w1_ref[...], preferred_element_type=F32)
    hid = jnp.square(jnp.maximum(hid, 0.0)).astype(BF16)
    part = jnp.dot(hid, w2_ref[...], preferred_element_type=F32)

    @pl.when(j == 0)
    def _():
        acc_ref[...] = part

    @pl.when(j > 0)
    def _():
        acc_ref[...] += part

    @pl.when(j == last)
    def _():
        x2 = x_ref[...] + mod_ref[0, 5:6, :] * acc_ref[...]
        o_ref[...] = x2 * _rms_scale(x2) * gf_ref[...]


def _mlp(x1, mod, g, w1, w2, g_final, seq, tm=512, tf=512):
    t, d = x1.shape
    f = w1.shape[1]
    return pl.pallas_call(
        _mlp_kernel,
        grid=(t // tm, f // tf),
        in_specs=[
            pl.BlockSpec((tm, d), lambda i, j: (i, 0)),
            pl.BlockSpec((1, N_MOD, d), lambda i, j: (i * tm // seq, 0, 0)),
            pl.BlockSpec((1, d), lambda i, j: (0, 0)),
            pl.BlockSpec((d, tf), lambda i, j: (0, j)),
            pl.BlockSpec((tf, d), lambda i, j: (j, 0)),
            pl.BlockSpec((1, d), lambda i, j: (0, 0)),
        ],
        out_specs=pl.BlockSpec((tm, d), lambda i, j: (i, 0)),
        out_shape=jax.ShapeDtypeStruct((t, d), F32),
        scratch_shapes=[pltpu.VMEM((tm, d), BF16), pltpu.VMEM((tm, d), F32)],
        compiler_params=pltpu.CompilerParams(
            dimension_semantics=("arbitrary", "arbitrary"), vmem_limit_bytes=VMEM_LIMIT),
        name="mlp",
    )(x1, mod, g, w1, w2, g_final)


def kernel(x, c, w_ada, b_ada, g_norm_mix, w_in, w_conv, b_conv, w_rg_a, b_rg_a, w_rg_x, b_rg_x,
           lru_lambda, g_attn_out, g_lru_out, w_out, g_norm_mlp, w_mlp_in, w_mlp_out, g_norm_final):
    batch, seq, d = x.shape
    assert w_ada.shape[0] == 1, "single layer: the final norm is fused into the MLP kernel"
    xt = x.reshape(batch * seq, d)
    c_pad = jnp.pad(c.astype(F32), ((0, 8 - batch), (0, 0)))
    mod = _ada(c_pad, w_ada[0], b_ada)[:batch].reshape(batch, N_MOD, d)
    qkv, xrg = _inproj(xt, mod, g_norm_mix, w_in[0].astype(BF16), seq)
    o_attn = _attention(qkv, batch, seq)
    w_gates = jnp.concatenate([w_rg_a[0], w_rg_x[0]], axis=-1).astype(BF16)
    o_lru = _lru(xrg, w_conv[0], b_conv, w_gates, b_rg_a, b_rg_x, lru_lambda, batch, seq)
    x1 = _outproj(o_attn, o_lru, xt, mod, g_attn_out, g_lru_out, w_out[0].astype(BF16), seq)
    out = _mlp(x1, mod, g_norm_mlp, w_mlp_in[0].astype(BF16), w_mlp_out[0].astype(BF16),
               g_norm_final[None, :], seq)
    return out.reshape(batch, seq, d)
```

```python
import functools

import jax
import jax.numpy as jnp
from jax import lax
from jax.experimental import pallas as pl
from jax.experimental.pallas import tpu as pltpu

F32 = jnp.float32
BF16 = jnp.bfloat16

ATTN_HEADS = 8
HEAD_DIM = 128
LRU_HEADS = 8
LRU_BLOCK = 128
CONV_WIDTH = 4
LRU_C = 8.0
N_MOD = 6
EPS = 1e-6

LOG2_E = 1.4426950408889634
EXP2_UNDERFLOW = -151.0

VMEM_LIMIT = 56 * 1024 * 1024


def _softplus(x):
    return jnp.maximum(x, 0.0) + jnp.log1p(jnp.exp(-jnp.abs(x)))


def _rms_scale(x):
    return lax.rsqrt(jnp.mean(x * x, axis=-1, keepdims=True) + EPS)


def _ada_kernel(c_ref, w_ref, b_ref, o_ref):
    c = c_ref[...]
    c_act = c * jax.nn.sigmoid(c)
    o_ref[...] = jnp.dot(c_act.astype(BF16), w_ref[...].astype(BF16),
                         preferred_element_type=F32) + b_ref[...]


def _ada(c_pad, w_ada, b_ada, tn=1024):
    m, d = c_pad.shape
    n = w_ada.shape[1]
    return pl.pallas_call(
        _ada_kernel,
        grid=(n // tn,),
        in_specs=[
            pl.BlockSpec((m, d), lambda j: (0, 0)),
            pl.BlockSpec((d, tn), lambda j: (0, j)),
            pl.BlockSpec((1, tn), lambda j: (0, j)),
        ],
        out_specs=pl.BlockSpec((m, tn), lambda j: (0, j)),
        out_shape=jax.ShapeDtypeStruct((m, n), F32),
        compiler_params=pltpu.CompilerParams(
            dimension_semantics=("arbitrary",), vmem_limit_bytes=VMEM_LIMIT),
        name="ada",
    )(c_pad, w_ada, b_ada)


def _inproj_kernel(x_ref, mod_ref, g_ref, w_ref, o_ref, h_ref):
    @pl.when(pl.program_id(1) == 0)
    def _():
        x = x_ref[...]
        y = x * _rms_scale(x) * g_ref[...]
        sh = mod_ref[0, 0:1, :]
        sc = mod_ref[0, 1:2, :]
        h_ref[...] = (y * (1.0 + sc) + sh).astype(BF16)

    o_ref[...] = jnp.dot(h_ref[...], w_ref[...], preferred_element_type=F32).astype(BF16)


def _inproj(x2, mod, g, w_in, seq, tm=1024, tn=1024):
    t, d = x2.shape
    n = w_in.shape[1]
    return pl.pallas_call(
        _inproj_kernel,
        grid=(t // tm, n // tn),
        in_specs=[
            pl.BlockSpec((tm, d), lambda i, j: (i, 0)),
            pl.BlockSpec((1, N_MOD, d), lambda i, j: (i * tm // seq, 0, 0)),
            pl.BlockSpec((1, d), lambda i, j: (0, 0)),
            pl.BlockSpec((d, tn), lambda i, j: (0, j)),
        ],
        out_specs=pl.BlockSpec((tm, tn), lambda i, j: (i, j)),
        out_shape=jax.ShapeDtypeStruct((t, n), BF16),
        scratch_shapes=[pltpu.VMEM((tm, d), BF16)],
        compiler_params=pltpu.CompilerParams(
            dimension_semantics=("arbitrary", "arbitrary"), vmem_limit_bytes=VMEM_LIMIT),
        name="inproj",
    )(x2, mod, g, w_in)


def _attn_kernel(q_ref, k_ref, v_ref, u_ref, o_ref, c_ref, *, tq, nq):
    zscale = HEAD_DIM ** -0.5 * LOG2_E
    u = u_ref[...]
    rows = lax.broadcasted_iota(jnp.int32, (tq, tq), 0)
    cols = lax.broadcasted_iota(jnp.int32, (tq, tq), 1)
    causal = cols < rows

    def scores(qb, start):
        k = k_ref[pl.ds(start, tq), :]
        z = lax.dot_general(qb, k, (((1,), (1,)), ((), ())),
                            preferred_element_type=F32) * zscale
        neg = jnp.minimum(z, 0.0)
        r = neg - z
        t = jnp.log2(1.0 + jnp.exp2(neg + r))
        return z, r - t

    def weights(z, log_stay, c, mask):
        if mask is not None:
            log_stay = jnp.where(mask, log_stay, 0.0)
        hi = log_stay.astype(BF16)
        lo = (log_stay - hi.astype(F32)).astype(BF16)
        incl = jnp.dot(jnp.concatenate([hi, lo], axis=1), u, preferred_element_type=F32)
        e = incl + z if c is None else incl + (z + c)
        w = jnp.exp2(e)
        if mask is not None:
            w = jnp.where(mask, w, 0.0)
        return w.astype(BF16), incl[:, 0:1]

    for i in range(nq):
        lo_row = i * tq
        qb = q_ref[lo_row:lo_row + tq, :]
        z, ls = scores(qb, lo_row)
        w, c = weights(z, ls, None, causal)
        acc = jnp.dot(w, v_ref[lo_row:lo_row + tq, :], preferred_element_type=F32)
        if i > 0:
            z, ls = scores(qb, lo_row - tq)
            w, rs = weights(z, ls, c, None)
            acc = acc + jnp.dot(w, v_ref[lo_row - tq:lo_row, :], preferred_element_type=F32)
            c = c + rs
        o_ref[lo_row:lo_row + tq, :] = acc
        if i > 1:
            c_ref[i] = c

    if nq > 2:
        @pl.when(jnp.max(c_ref[2:nq]) > EXP2_UNDERFLOW)
        def _():
            for i in range(2, nq):
                lo_row = i * tq
                qb = q_ref[lo_row:lo_row + tq, :]

                def cond(state):
                    j, alive, _ = state
                    return jnp.logical_and(j >= 0, alive)

                def body(state):
                    j, _, c = state
                    start = pl.multiple_of(j * tq, tq)
                    z, ls = scores(qb, start)
                    w, rs = weights(z, ls, c, None)
                    o_ref[lo_row:lo_row + tq, :] += jnp.dot(
                        w, v_ref[pl.ds(start, tq), :], preferred_element_type=F32)
                    c = c + rs
                    return j - 1, jnp.max(c) > EXP2_UNDERFLOW, c

                c0 = c_ref[i]
                lax.while_loop(cond, body, (i - 2, jnp.max(c0) > EXP2_UNDERFLOW, c0))


def _attention(proj, batch, seq, tq=256):
    t = proj.shape[0]
    nq = seq // tq
    r = lax.broadcasted_iota(jnp.int32, (2 * tq, tq), 0) % tq
    s = lax.broadcasted_iota(jnp.int32, (2 * tq, tq), 1)
    u = (r >= s).astype(BF16)
    h = ATTN_HEADS
    head = lambda off: pl.BlockSpec((seq, HEAD_DIM), lambda b, hh: (b, off + hh))
    return pl.pallas_call(
        functools.partial(_attn_kernel, tq=tq, nq=nq),
        grid=(batch, h),
        in_specs=[head(0), head(h), head(2 * h), pl.BlockSpec((2 * tq, tq), lambda b, hh: (0, 0))],
        out_specs=head(0),
        out_shape=jax.ShapeDtypeStruct((t, h * HEAD_DIM), F32),
        scratch_shapes=[pltpu.VMEM((nq, tq, 1), F32)],
        compiler_params=pltpu.CompilerParams(
            dimension_semantics=("arbitrary", "arbitrary"), vmem_limit_bytes=VMEM_LIMIT),
        name="attn",
    )(proj, proj, proj, u)


def _lru_kernel(xr_ref, xg_ref, wc_ref, bc_ref, wg_ref, ba_ref, bx_ref, lam_ref,
                o_ref, ext_ref, hc_ref, *, ts):
    s = pl.program_id(1)
    hist = 8

    @pl.when(s == 0)
    def _():
        ext_ref[0:hist, :] = jnp.zeros((hist, ext_ref.shape[1]), F32)
        hc_ref[...] = jnp.zeros_like(hc_ref)

    ext_ref[hist:hist + ts, :] = xr_ref[...].astype(F32)
    nv = ts // 8
    sub = lax.broadcasted_iota(jnp.int32, (nv, 8, LRU_BLOCK), 1)

    for h in range(LRU_HEADS):
        sl = slice(h * LRU_BLOCK, (h + 1) * LRU_BLOCK)
        wc = wc_ref[:, sl]
        xc = bc_ref[:, sl]
        for tap in range(CONV_WIDTH):
            off = hist - (CONV_WIDTH - 1) + tap
            xc = xc + wc[tap:tap + 1, :] * ext_ref[off:off + ts, sl]
        gates = jnp.dot(xc.astype(BF16), wg_ref[h], preferred_element_type=F32)
        r = jax.nn.sigmoid(gates[:, :LRU_BLOCK] + ba_ref[:, sl])
        gi = jax.nn.sigmoid(gates[:, LRU_BLOCK:] + bx_ref[:, sl])
        log_a = (-LRU_C) * r * _softplus(-lam_ref[:, sl])
        a = jnp.exp(log_a)
        th = jnp.tanh(log_a)
        b = jnp.sqrt(-2.0 * th / (1.0 - th)) * (gi * xc)
        a = a.reshape(nv, 8, LRU_BLOCK)
        b = b.reshape(nv, 8, LRU_BLOCK)
        for d in (1, 2, 4):
            keep = sub >= d
            a_prev = pltpu.roll(a, d, 1)
            b_prev = pltpu.roll(b, d, 1)
            b = b + a * jnp.where(keep, b_prev, 0.0)
            a = a * jnp.where(keep, a_prev, 1.0)
        gate = jax.nn.gelu(xg_ref[:, sl].astype(F32), approximate=True).reshape(nv, 8, LRU_BLOCK)
        carry = hc_ref[0:1, sl]
        for i in range(nv):
            hcur = b[i] + a[i] * carry
            o_ref[8 * i:8 * i + 8, sl] = hcur * gate[i]
            carry = hcur[7:8, :]
        hc_ref[0:1, sl] = carry

    ext_ref[0:hist, :] = ext_ref[ts:ts + hist, :]


def _lru(proj, w_conv, b_conv, w_gates, b_a, b_x, lam, batch, seq, ts=256):
    t = proj.shape[0]
    c = LRU_HEADS * LRU_BLOCK
    ns = seq // ts
    vec = pl.BlockSpec((1, c), lambda b, s: (0, 0))
    xr_col = 3 * ATTN_HEADS * HEAD_DIM // c
    return pl.pallas_call(
        functools.partial(_lru_kernel, ts=ts),
        grid=(batch, ns),
        in_specs=[
            pl.BlockSpec((ts, c), lambda b, s: (b * ns + s, xr_col)),
            pl.BlockSpec((ts, c), lambda b, s: (b * ns + s, xr_col + 1)),
            pl.BlockSpec((CONV_WIDTH, c), lambda b, s: (0, 0)),
            vec,
            pl.BlockSpec((LRU_HEADS, LRU_BLOCK, 2 * LRU_BLOCK), lambda b, s: (0, 0, 0)),
            vec, vec, vec,
        ],
        out_specs=pl.BlockSpec((ts, c), lambda b, s: (b * ns + s, 0)),
        out_shape=jax.ShapeDtypeStruct((t, c), F32),
        scratch_shapes=[pltpu.VMEM((ts + 8, c), F32), pltpu.VMEM((8, c), F32)],
        compiler_params=pltpu.CompilerParams(
            dimension_semantics=("arbitrary", "arbitrary"), vmem_limit_bytes=VMEM_LIMIT),
        name="lru",
    )(proj, proj, w_conv, b_conv, w_gates, b_a, b_x, lam)


def _outproj_kernel(oa_ref, ol_ref, x_ref, mod_ref, ga_ref, gl_ref, w_ref, o_ref):
    oa = oa_ref[...]
    ol = ol_ref[...]
    na = (oa * _rms_scale(oa) * ga_ref[...]).astype(BF16)
    nl = (ol * _rms_scale(ol) * gl_ref[...]).astype(BF16)
    half = oa.shape[1]
    y = (jnp.dot(na, w_ref[0:half, :], preferred_element_type=F32)
         + jnp.dot(nl, w_ref[half:2 * half, :], preferred_element_type=F32))
    o_ref[...] = x_ref[...] + mod_ref[0, 2:3, :] * y


def _outproj(o_attn, o_lru, x2, mod, g_a, g_l, w_out, seq, tm=512):
    t, d = x2.shape
    half = o_attn.shape[1]
    return pl.pallas_call(
        _outproj_kernel,
        grid=(t // tm,),
        in_specs=[
            pl.BlockSpec((tm, half), lambda i: (i, 0)),
            pl.BlockSpec((tm, half), lambda i: (i, 0)),
            pl.BlockSpec((tm, d), lambda i: (i, 0)),
            pl.BlockSpec((1, N_MOD, d), lambda i: (i * tm // seq, 0, 0)),
            pl.BlockSpec((1, half), lambda i: (0, 0)),
            pl.BlockSpec((1, half), lambda i: (0, 0)),
            pl.BlockSpec((2 * half, d), lambda i: (0, 0)),
        ],
        out_specs=pl.BlockSpec((tm, d), lambda i: (i, 0)),
        out_shape=jax.ShapeDtypeStruct((t, d), F32),
        compiler_params=pltpu.CompilerParams(
            dimension_semantics=("arbitrary",), vmem_limit_bytes=VMEM_LIMIT),
        name="outproj",
    )(o_attn, o_lru, x2, mod, g_a, g_l, w_out)


def _mlp_kernel(x_ref, mod_ref, g_ref, w1_ref, w2_ref, gf_ref, o_ref, h_ref):
    j = pl.program_id(1)

    @pl.when(j == 0)
    def _():
        x = x_ref[...]
        y = x * _rms_scale(x) * g_ref[...]
        h_ref[...] = (y * (1.0 + mod_ref[0, 4:5, :]) + mod_ref[0, 3:4, :]).astype(BF16)
        o_ref[...] = jnp.zeros_like(o_ref)

    hid = jnp.dot(h_ref[...], w1_ref[...], preferred_element_type=F32)
    hid = jnp.square(jnp.maximum(hid, 0.0)).astype(BF16)
    o_ref[...] += jnp.dot(hid, w2_ref[...], preferred_element_type=F32)

    @pl.when(j == pl.num_programs(1) - 1)
    def _():
        x2 = x_ref[...] + mod_ref[0, 5:6, :] * o_ref[...]
        o_ref[...] = x2 * _rms_scale(x2) * gf_ref[...]


def _mlp(x1, mod, g, w1, w2, g_final, seq, tm=512, tf=1024):
    t, d = x1.shape
    f = w1.shape[1]
    return pl.pallas_call(
        _mlp_kernel,
        grid=(t // tm, f // tf),
        in_specs=[
            pl.BlockSpec((tm, d), lambda i, j: (i, 0)),
            pl.BlockSpec((1, N_MOD, d), lambda i, j: (i * tm // seq, 0, 0)),
            pl.BlockSpec((1, d), lambda i, j: (0, 0)),
            pl.BlockSpec((d, tf), lambda i, j: (0, j)),
            pl.BlockSpec((tf, d), lambda i, j: (j, 0)),
            pl.BlockSpec((1, d), lambda i, j: (0, 0)),
        ],
        out_specs=pl.BlockSpec((tm, d), lambda i, j: (i, 0)),
        out_shape=jax.ShapeDtypeStruct((t, d), F32),
        scratch_shapes=[pltpu.VMEM((tm, d), BF16)],
        compiler_params=pltpu.CompilerParams(
            dimension_semantics=("arbitrary", "arbitrary"), vmem_limit_bytes=VMEM_LIMIT),
        name="mlp",
    )(x1, mod, g, w1, w2, g_final)


def kernel(x, c, w_ada, b_ada, g_norm_mix, w_in, w_conv, b_conv, w_rg_a, b_rg_a, w_rg_x, b_rg_x,
           lru_lambda, g_attn_out, g_lru_out, w_out, g_norm_mlp, w_mlp_in, w_mlp_out, g_norm_final):
    batch, seq, d = x.shape
    assert w_ada.shape[0] == 1, "single layer: the final norm is fused into the MLP kernel"
    xt = x.reshape(batch * seq, d)
    c_pad = jnp.pad(c.astype(F32), ((0, 8 - batch), (0, 0)))
    mod = _ada(c_pad, w_ada[0], b_ada)[:batch].reshape(batch, N_MOD, d)
    proj = _inproj(xt, mod, g_norm_mix, w_in[0].astype(BF16), seq)
    o_attn = _attention(proj, batch, seq)
    w_gates = jnp.concatenate([w_rg_a[0], w_rg_x[0]], axis=-1).astype(BF16)
    o_lru = _lru(proj, w_conv[0], b_conv, w_gates, b_rg_a, b_rg_x, lru_lambda, batch, seq)
    x1 = _outproj(o_attn, o_lru, xt, mod, g_attn_out, g_lru_out, w_out[0].astype(BF16), seq)
    out = _mlp(x1, mod, g_norm_mlp, w_mlp_in[0].astype(BF16), w_mlp_out[0].astype(BF16),
               g_norm_final[None, :], seq)
    return out.reshape(batch, seq, d)
```

```python
import functools

import jax
import jax.numpy as jnp
from jax import lax
from jax.experimental import pallas as pl
from jax.experimental.pallas import tpu as pltpu

F32 = jnp.float32
BF16 = jnp.bfloat16

ATTN_HEADS = 8
HEAD_DIM = 128
LRU_HEADS = 8
LRU_BLOCK = 128
CONV_WIDTH = 4
LRU_C = 8.0
N_MOD = 6
EPS = 1e-6

LOG2_E = 1.4426950408889634
EXP2_UNDERFLOW = -151.0

VMEM_LIMIT = 56 * 1024 * 1024


def _softplus(x):
    return jnp.maximum(x, 0.0) + jnp.log1p(jnp.exp(-jnp.abs(x)))


def _rms_scale(x):
    return lax.rsqrt(jnp.mean(x * x, axis=-1, keepdims=True) + EPS)


def _ada_kernel(c_ref, w_ref, b_ref, o_ref):
    c = c_ref[...]
    c_act = c * jax.nn.sigmoid(c)
    o_ref[...] = jnp.dot(c_act.astype(BF16), w_ref[...].astype(BF16),
                         preferred_element_type=F32) + b_ref[...]


def _ada(c_pad, w_ada, b_ada, tn=1024):
    m, d = c_pad.shape
    n = w_ada.shape[1]
    return pl.pallas_call(
        _ada_kernel,
        grid=(n // tn,),
        in_specs=[
            pl.BlockSpec((m, d), lambda j: (0, 0)),
            pl.BlockSpec((d, tn), lambda j: (0, j)),
            pl.BlockSpec((1, tn), lambda j: (0, j)),
        ],
        out_specs=pl.BlockSpec((m, tn), lambda j: (0, j)),
        out_shape=jax.ShapeDtypeStruct((m, n), F32),
        compiler_params=pltpu.CompilerParams(
            dimension_semantics=("arbitrary",), vmem_limit_bytes=VMEM_LIMIT),
        name="ada",
    )(c_pad, w_ada, b_ada)


def _inproj_kernel(x_ref, mod_ref, g_ref, w_ref, o_ref, h_ref):
    @pl.when(pl.program_id(1) == 0)
    def _():
        x = x_ref[...]
        y = x * _rms_scale(x) * g_ref[...]
        sh = mod_ref[0, 0:1, :]
        sc = mod_ref[0, 1:2, :]
        h_ref[...] = (y * (1.0 + sc) + sh).astype(BF16)

    o_ref[...] = jnp.dot(h_ref[...], w_ref[...], preferred_element_type=F32).astype(BF16)


def _inproj(x2, mod, g, w_in, seq, tm=1024, tn=1024):
    t, d = x2.shape
    n = w_in.shape[1]
    return pl.pallas_call(
        _inproj_kernel,
        grid=(t // tm, n // tn),
        in_specs=[
            pl.BlockSpec((tm, d), lambda i, j: (i, 0)),
            pl.BlockSpec((1, N_MOD, d), lambda i, j: (i * tm // seq, 0, 0)),
            pl.BlockSpec((1, d), lambda i, j: (0, 0)),
            pl.BlockSpec((d, tn), lambda i, j: (0, j)),
        ],
        out_specs=pl.BlockSpec((tm, tn), lambda i, j: (i, j)),
        out_shape=jax.ShapeDtypeStruct((t, n), BF16),
        scratch_shapes=[pltpu.VMEM((tm, d), BF16)],
        compiler_params=pltpu.CompilerParams(
            dimension_semantics=("arbitrary", "arbitrary"), vmem_limit_bytes=VMEM_LIMIT),
        name="inproj",
    )(x2, mod, g, w_in)


def _attn_kernel(q_ref, k_ref, v_ref, u_ref, w1f_ref, wof_ref, o_ref, w1b_ref, wob_ref, c_ref,
                 *, tq, nq):
    w1b_ref[...] = w1f_ref[...].astype(BF16)
    wob_ref[...] = wof_ref[...].astype(BF16)

    zscale = HEAD_DIM ** -0.5 * LOG2_E
    u = u_ref[...]
    rows = lax.broadcasted_iota(jnp.int32, (tq, tq), 0)
    cols = lax.broadcasted_iota(jnp.int32, (tq, tq), 1)
    causal = cols < rows

    def scores(qb, start):
        k = k_ref[pl.ds(start, tq), :]
        z = lax.dot_general(qb, k, (((1,), (1,)), ((), ())),
                            preferred_element_type=F32) * zscale
        neg = jnp.minimum(z, 0.0)
        r = neg - z
        t = jnp.log2(1.0 + jnp.exp2(neg + r))
        return z, r - t

    def weights(z, log_stay, c, mask):
        if mask is not None:
            log_stay = jnp.where(mask, log_stay, 0.0)
        hi = log_stay.astype(BF16)
        lo = (log_stay - hi.astype(F32)).astype(BF16)
        incl = jnp.dot(jnp.concatenate([hi, lo], axis=1), u, preferred_element_type=F32)
        e = incl + z if c is None else incl + (z + c)
        w = jnp.exp2(e)
        if mask is not None:
            w = jnp.where(mask, w, 0.0)
        return w.astype(BF16), incl[:, 0:1]

    for i in range(nq):
        lo_row = i * tq
        qb = q_ref[lo_row:lo_row + tq, :]
        z, ls = scores(qb, lo_row)
        w, c = weights(z, ls, None, causal)
        acc = jnp.dot(w, v_ref[lo_row:lo_row + tq, :], preferred_element_type=F32)
        if i > 0:
            z, ls = scores(qb, lo_row - tq)
            w, rs = weights(z, ls, c, None)
            acc = acc + jnp.dot(w, v_ref[lo_row - tq:lo_row, :], preferred_element_type=F32)
            c = c + rs
        o_ref[lo_row:lo_row + tq, :] = acc
        if i > 1:
            c_ref[i] = c

    if nq > 2:
        @pl.when(jnp.max(c_ref[2:nq]) > EXP2_UNDERFLOW)
        def _():
            for i in range(2, nq):
                lo_row = i * tq
                qb = q_ref[lo_row:lo_row + tq, :]

                def cond(state):
                    j, alive, _ = state
                    return jnp.logical_and(j >= 0, alive)

                def body(state):
                    j, _, c = state
                    start = pl.multiple_of(j * tq, tq)
                    z, ls = scores(qb, start)
                    w, rs = weights(z, ls, c, None)
                    o_ref[lo_row:lo_row + tq, :] += jnp.dot(
                        w, v_ref[pl.ds(start, tq), :], preferred_element_type=F32)
                    c = c + rs
                    return j - 1, jnp.max(c) > EXP2_UNDERFLOW, c

                c0 = c_ref[i]
                lax.while_loop(cond, body, (i - 2, jnp.max(c0) > EXP2_UNDERFLOW, c0))


def _attention(proj, w_mlp_in, w_out, batch, seq, tq=256):
    t = proj.shape[0]
    nq = seq // tq
    r = lax.broadcasted_iota(jnp.int32, (2 * tq, tq), 0) % tq
    s = lax.broadcasted_iota(jnp.int32, (2 * tq, tq), 1)
    u = (r >= s).astype(BF16)
    h = ATTN_HEADS
    steps = batch * h
    d, f = w_mlp_in.shape
    w1_chunk = pl.BlockSpec((d, f // steps), lambda b, hh: (0, b * h + hh))
    wo_chunk = pl.BlockSpec((w_out.shape[0] // steps, w_out.shape[1]), lambda b, hh: (b * h + hh, 0))
    head = lambda off: pl.BlockSpec((seq, HEAD_DIM), lambda b, hh: (b, off + hh))
    return pl.pallas_call(
        functools.partial(_attn_kernel, tq=tq, nq=nq),
        grid=(batch, h),
        in_specs=[head(0), head(h), head(2 * h), pl.BlockSpec((2 * tq, tq), lambda b, hh: (0, 0)),
                  w1_chunk, wo_chunk],
        out_specs=[head(0), w1_chunk, wo_chunk],
        out_shape=[jax.ShapeDtypeStruct((t, h * HEAD_DIM), F32),
                   jax.ShapeDtypeStruct(w_mlp_in.shape, BF16),
                   jax.ShapeDtypeStruct(w_out.shape, BF16)],
        scratch_shapes=[pltpu.VMEM((nq, tq, 1), F32)],
        compiler_params=pltpu.CompilerParams(
            dimension_semantics=("arbitrary", "arbitrary"), vmem_limit_bytes=VMEM_LIMIT),
        name="attn",
    )(proj, proj, proj, u, w_mlp_in, w_out)


def _mix_kernel(xr_ref, xg_ref, oa_ref, x_ref, mod_ref, wc_ref, bc_ref, wg_ref, ba_ref, bx_ref,
                lam_ref, ga_ref, gl_ref, wo_ref, w2f_ref, o_ref, w2b_ref,
                ext_ref, hc_ref, ol_ref, nl_ref, *, ts, ns):
    g = pl.program_id(0)
    hist = 8
    half = oa_ref.shape[1]

    w2b_ref[...] = w2f_ref[...].astype(BF16)

    @pl.when(g % ns == 0)
    def _():
        ext_ref[0:hist, :] = jnp.zeros((hist, ext_ref.shape[1]), F32)
        hc_ref[...] = jnp.zeros_like(hc_ref)

    @pl.when(g == 0)
    def _():
        nl_ref[...] = jnp.zeros_like(nl_ref)

    oa = oa_ref[...]
    na = (oa * _rms_scale(oa) * ga_ref[...]).astype(BF16)
    mixed = jnp.concatenate([na, nl_ref[(g + 1) % 2]], axis=1)
    ncol = o_ref.shape[1] // LRU_HEADS

    def project(h):
        cs = slice(h * ncol, (h + 1) * ncol)
        y = jnp.dot(mixed, wo_ref[:, cs], preferred_element_type=F32)
        o_ref[:, cs] = x_ref[:, cs] + mod_ref[0, 2:3, cs] * y

    ext_ref[hist:hist + ts, :] = xr_ref[...].astype(F32)
    nv = ts // 8
    sub = lax.broadcasted_iota(jnp.int32, (nv, 8, LRU_BLOCK), 1)

    for h in range(LRU_HEADS):
        project(h)
        sl = slice(h * LRU_BLOCK, (h + 1) * LRU_BLOCK)
        wc = wc_ref[:, sl]
        xe = ext_ref[0:hist + ts, sl].reshape(nv + 1, 8, LRU_BLOCK)
        xc = bc_ref[:, sl] + wc[CONV_WIDTH - 1:CONV_WIDTH, :] * xe[1:]
        for d in range(1, CONV_WIDTH):
            rolled = pltpu.roll(xe, d, 1)
            tap = CONV_WIDTH - 1 - d
            xc = xc + wc[tap:tap + 1, :] * jnp.where(sub >= d, rolled[1:], rolled[:-1])
        xc = xc.reshape(ts, LRU_BLOCK)
        gates = jnp.dot(xc.astype(BF16), wg_ref[h], preferred_element_type=F32)
        r = jax.nn.sigmoid(gates[:, :LRU_BLOCK] + ba_ref[:, sl])
        gi = jax.nn.sigmoid(gates[:, LRU_BLOCK:] + bx_ref[:, sl])
        log_a = (-LRU_C) * r * _softplus(-lam_ref[:, sl])
        a = jnp.exp(log_a)
        one_minus_a2 = -jnp.tanh(log_a) * (1.0 + a * a)
        b = jnp.exp2(jnp.log(one_minus_a2) * (0.5 * LOG2_E)) * (gi * xc)
        a = a.reshape(nv, 8, LRU_BLOCK)
        b = b.reshape(nv, 8, LRU_BLOCK)
        for d in (1, 2, 4):
            keep = sub >= d
            a_prev = pltpu.roll(a, d, 1)
            b_prev = pltpu.roll(b, d, 1)
            b = b + a * jnp.where(keep, b_prev, 0.0)
            a = a * jnp.where(keep, a_prev, 1.0)
        gate = jax.nn.gelu(xg_ref[:, sl].astype(F32), approximate=True).reshape(nv, 8, LRU_BLOCK)
        carry = hc_ref[0:1, sl]
        for i in range(nv):
            hcur = b[i] + a[i] * carry
            ol_ref[8 * i:8 * i + 8, sl] = hcur * gate[i]
            carry = hcur[7:8, :]
        hc_ref[0:1, sl] = carry

    ext_ref[0:hist, :] = ext_ref[ts:ts + hist, :]
    ol = ol_ref[...]
    nl_ref[g % 2] = (ol * _rms_scale(ol) * gl_ref[...]).astype(BF16)


def _mix(proj, o_attn, x2, mod, w_conv, b_conv, w_gates, b_a, b_x, lam, g_a, g_l, w_out, w_mlp_out,
         seq, ts=256):
    t, d = x2.shape
    c = LRU_HEADS * LRU_BLOCK
    ns = seq // ts
    nblk = t // ts
    cur = lambda g: jnp.minimum(g, nblk - 1)
    prev = lambda g: jnp.maximum(g - 1, 0)
    vec = pl.BlockSpec((1, c), lambda g: (0, 0))
    xr_col = 3 * ATTN_HEADS * HEAD_DIM // c
    f = w_mlp_out.shape[0]
    w2_chunk = pl.BlockSpec((f // nblk, d), lambda g: (cur(g), 0))
    return pl.pallas_call(
        functools.partial(_mix_kernel, ts=ts, ns=ns),
        grid=(nblk + 1,),
        in_specs=[
            pl.BlockSpec((ts, c), lambda g: (cur(g), xr_col)),
            pl.BlockSpec((ts, c), lambda g: (cur(g), xr_col + 1)),
            pl.BlockSpec((ts, c), lambda g: (prev(g), 0)),
            pl.BlockSpec((ts, d), lambda g: (prev(g), 0)),
            pl.BlockSpec((1, N_MOD, d), lambda g: (prev(g) // ns, 0, 0)),
            pl.BlockSpec((CONV_WIDTH, c), lambda g: (0, 0)),
            vec,
            pl.BlockSpec((LRU_HEADS, LRU_BLOCK, 2 * LRU_BLOCK), lambda g: (0, 0, 0)),
            vec, vec, vec, vec, vec,
            pl.BlockSpec((2 * c, d), lambda g: (0, 0)),
            w2_chunk,
        ],
        out_specs=[pl.BlockSpec((ts, d), lambda g: (prev(g), 0)), w2_chunk],
        out_shape=[jax.ShapeDtypeStruct((t, d), F32), jax.ShapeDtypeStruct(w_mlp_out.shape, BF16)],
        scratch_shapes=[pltpu.VMEM((ts + 8, c), F32), pltpu.VMEM((8, c), F32),
                        pltpu.VMEM((ts, c), F32), pltpu.VMEM((2, ts, c), BF16)],
        compiler_params=pltpu.CompilerParams(
            dimension_semantics=("arbitrary",), vmem_limit_bytes=VMEM_LIMIT),
        name="mix",
    )(proj, proj, o_attn, x2, mod, w_conv, b_conv, w_gates, b_a, b_x, lam, g_a, g_l, w_out, w_mlp_out)


def _mlp_kernel(x_ref, mod_ref, g_ref, w1_ref, w2_ref, gf_ref, o_ref, h_ref):
    j = pl.program_id(1)

    @pl.when(j == 0)
    def _():
        x = x_ref[...]
        y = x * _rms_scale(x) * g_ref[...]
        h_ref[...] = (y * (1.0 + mod_ref[0, 4:5, :]) + mod_ref[0, 3:4, :]).astype(BF16)
        o_ref[...] = jnp.zeros_like(o_ref)

    hid = jnp.dot(h_ref[...], w1_ref[...], preferred_element_type=F32)
    hid = jnp.square(jnp.maximum(hid, 0.0)).astype(BF16)
    o_ref[...] += jnp.dot(hid, w2_ref[...], preferred_element_type=F32)

    @pl.when(j == pl.num_programs(1) - 1)
    def _():
        x2 = x_ref[...] + mod_ref[0, 5:6, :] * o_ref[...]
        o_ref[...] = x2 * _rms_scale(x2) * gf_ref[...]


def _mlp(x1, mod, g, w1, w2, g_final, seq, tm=512, tf=1024):
    t, d = x1.shape
    f = w1.shape[1]
    return pl.pallas_call(
        _mlp_kernel,
        grid=(t // tm, f // tf),
        in_specs=[
            pl.BlockSpec((tm, d), lambda i, j: (i, 0)),
            pl.BlockSpec((1, N_MOD, d), lambda i, j: (i * tm // seq, 0, 0)),
            pl.BlockSpec((1, d), lambda i, j: (0, 0)),
            pl.BlockSpec((d, tf), lambda i, j: (0, j)),
            pl.BlockSpec((tf, d), lambda i, j: (j, 0)),
            pl.BlockSpec((1, d), lambda i, j: (0, 0)),
        ],
        out_specs=pl.BlockSpec((tm, d), lambda i, j: (i, 0)),
        out_shape=jax.ShapeDtypeStruct((t, d), F32),
        scratch_shapes=[pltpu.VMEM((tm, d), BF16)],
        compiler_params=pltpu.CompilerParams(
            dimension_semantics=("arbitrary", "arbitrary"), vmem_limit_bytes=VMEM_LIMIT),
        name="mlp",
    )(x1, mod, g, w1, w2, g_final)


def kernel(x, c, w_ada, b_ada, g_norm_mix, w_in, w_conv, b_conv, w_rg_a, b_rg_a, w_rg_x, b_rg_x,
           lru_lambda, g_attn_out, g_lru_out, w_out, g_norm_mlp, w_mlp_in, w_mlp_out, g_norm_final):
    batch, seq, d = x.shape
    assert w_ada.shape[0] == 1, "single layer: the final norm is fused into the MLP kernel"
    xt = x.reshape(batch * seq, d)
    c_pad = jnp.pad(c.astype(F32), ((0, 8 - batch), (0, 0)))
    mod = _ada(c_pad, w_ada[0], b_ada)[:batch].reshape(batch, N_MOD, d)
    proj = _inproj(xt, mod, g_norm_mix, w_in[0].astype(BF16), seq)
    o_attn, w1, wo = _attention(proj, w_mlp_in[0], w_out[0], batch, seq)
    w_gates = jnp.concatenate([w_rg_a[0], w_rg_x[0]], axis=-1).astype(BF16)
    x1, w2 = _mix(proj, o_attn, xt, mod, w_conv[0], b_conv, w_gates, b_rg_a, b_rg_x, lru_lambda,
                  g_attn_out, g_lru_out, wo, w_mlp_out[0], seq)
    out = _mlp(x1, mod, g_norm_mlp, w1, w2, g_norm_final[None, :], seq)
    return out.reshape(batch, seq, d)
```

```python
import functools

import jax
import jax.numpy as jnp
from jax import lax
from jax.experimental import pallas as pl
from jax.experimental.pallas import tpu as pltpu

F32 = jnp.float32
BF16 = jnp.bfloat16

ATTN_HEADS = 8
HEAD_DIM = 128
LRU_HEADS = 8
LRU_BLOCK = 128
CONV_WIDTH = 4
LRU_C = 8.0
N_MOD = 6
EPS = 1e-6

LOG2_E = 1.4426950408889634
EXP2_UNDERFLOW = -151.0

VMEM_LIMIT = 56 * 1024 * 1024
SCORES_AHEAD = 2


def _softplus(x):
    return jnp.maximum(x, 0.0) + jnp.log1p(jnp.exp(-jnp.abs(x)))


def _rms_scale(x):
    return lax.rsqrt(jnp.mean(x * x, axis=-1, keepdims=True) + EPS)


def _ada_kernel(c_ref, w_ref, b_ref, o_ref):
    c = c_ref[...]
    c_act = c * jax.nn.sigmoid(c)
    o_ref[...] = jnp.dot(c_act.astype(BF16), w_ref[...].astype(BF16),
                         preferred_element_type=F32) + b_ref[...]


def _ada(c_pad, w_ada, b_ada, tn=1024):
    m, d = c_pad.shape
    n = w_ada.shape[1]
    return pl.pallas_call(
        _ada_kernel,
        grid=(n // tn,),
        in_specs=[
            pl.BlockSpec((m, d), lambda j: (0, 0)),
            pl.BlockSpec((d, tn), lambda j: (0, j)),
            pl.BlockSpec((1, tn), lambda j: (0, j)),
        ],
        out_specs=pl.BlockSpec((m, tn), lambda j: (0, j)),
        out_shape=jax.ShapeDtypeStruct((m, n), F32),
        compiler_params=pltpu.CompilerParams(
            dimension_semantics=("arbitrary",), vmem_limit_bytes=VMEM_LIMIT),
        name="ada",
    )(c_pad, w_ada, b_ada)


def _adaln(x, g, shift, scale):
    return ((x * _rms_scale(x) * g) * (1.0 + scale) + shift).astype(BF16)


def _adaln_kernel(x_ref, mod_ref, g_ref, o_ref, *, row):
    o_ref[...] = _adaln(x_ref[...], g_ref[...], mod_ref[0, row:row + 1, :],
                        mod_ref[0, row + 1:row + 2, :])


def _adaln_first_tile(x2, mod, g, row, tm, rc=256):
    d = x2.shape[1]
    return pl.pallas_call(
        functools.partial(_adaln_kernel, row=row),
        grid=(tm // rc,),
        in_specs=[
            pl.BlockSpec((rc, d), lambda r: (r, 0)),
            pl.BlockSpec((1, N_MOD, d), lambda r: (0, 0, 0)),
            pl.BlockSpec((1, d), lambda r: (0, 0)),
        ],
        out_specs=pl.BlockSpec((rc, d), lambda r: (r, 0)),
        out_shape=jax.ShapeDtypeStruct((tm, d), BF16),
        compiler_params=pltpu.CompilerParams(
            dimension_semantics=("arbitrary",), vmem_limit_bytes=VMEM_LIMIT),
        name="adaln_first",
    )(x2, mod, g)


def _with_staged_tiles(h0_ref, xc_ref, mod_ref, g_ref, ha_ref, hb_ref, row, consume):
    i, j = pl.program_id(0), pl.program_id(1)

    @pl.when(jnp.logical_and(i == 0, j == 0))
    def _():
        ha_ref[...] = h0_ref[...]

    rc = xc_ref.shape[0]
    rows = pl.ds(pl.multiple_of(j * rc, rc), rc)

    def step(cur_ref, nxt_ref):
        consume(cur_ref)
        nxt_ref[rows, :] = _adaln(xc_ref[...], g_ref[...], mod_ref[0, row:row + 1, :],
                                  mod_ref[0, row + 1:row + 2, :])

    @pl.when(i % 2 == 0)
    def _():
        step(ha_ref, hb_ref)

    @pl.when(i % 2 == 1)
    def _():
        step(hb_ref, ha_ref)


def _next_tile_specs(tm, nj, nt, d, seq):
    nxt = lambda i: jnp.minimum(i + 1, nt - 1)
    return [pl.BlockSpec((tm // nj, d), lambda i, j: (nxt(i) * nj + j, 0)),
            pl.BlockSpec((1, N_MOD, d), lambda i, j: (nxt(i) * tm // seq, 0, 0))]


def _inproj_kernel(h0_ref, xc_ref, mod_ref, g_ref, w_ref, o_ref, ha_ref, hb_ref):
    def consume(h_ref):
        o_ref[...] = jnp.dot(h_ref[...], w_ref[...], preferred_element_type=F32).astype(BF16)

    _with_staged_tiles(h0_ref, xc_ref, mod_ref, g_ref, ha_ref, hb_ref, 0, consume)


def _inproj(x2, mod, g, w_in, seq, tm=1024, tn=1280):
    t, d = x2.shape
    n = w_in.shape[1]
    nt, nj = t // tm, n // tn
    h0 = _adaln_first_tile(x2, mod, g, 0, tm)
    return pl.pallas_call(
        _inproj_kernel,
        grid=(nt, nj),
        in_specs=[pl.BlockSpec((tm, d), lambda i, j: (0, 0))] + _next_tile_specs(tm, nj, nt, d, seq) + [
            pl.BlockSpec((1, d), lambda i, j: (0, 0)),
            pl.BlockSpec((d, tn), lambda i, j: (0, j)),
        ],
        out_specs=pl.BlockSpec((tm, tn), lambda i, j: (i, j)),
        out_shape=jax.ShapeDtypeStruct((t, n), BF16),
        scratch_shapes=[pltpu.VMEM((tm, d), BF16), pltpu.VMEM((tm, d), BF16)],
        compiler_params=pltpu.CompilerParams(
            dimension_semantics=("arbitrary", "arbitrary"), vmem_limit_bytes=VMEM_LIMIT),
        name="inproj",
    )(h0, x2, mod, g, w_in)


def _attn_kernel(q_ref, k_ref, v_ref, u_ref, w1f_ref, wof_ref, o_ref, w1b_ref, wob_ref, c_ref,
                 *, tq, nq):
    w1b_ref[...] = w1f_ref[...].astype(BF16)
    wob_ref[...] = wof_ref[...].astype(BF16)

    zscale = HEAD_DIM ** -0.5 * LOG2_E
    u = u_ref[...]
    rows = lax.broadcasted_iota(jnp.int32, (tq, tq), 0)
    cols = lax.broadcasted_iota(jnp.int32, (tq, tq), 1)
    causal = cols < rows

    def scores(qb, start):
        k = k_ref[pl.ds(start, tq), :]
        z = lax.dot_general(qb, k, (((1,), (1,)), ((), ())),
                            preferred_element_type=F32) * zscale
        neg = jnp.minimum(z, 0.0)
        r = neg - z
        t = jnp.log2(1.0 + jnp.exp2(neg + r))
        return neg - t, r - t

    def weights(log_beta, log_stay, c, mask):
        if mask is not None:
            log_stay = jnp.where(mask, log_stay, 0.0)
        after = jnp.dot(log_stay.astype(BF16), u, preferred_element_type=F32)
        e = after + log_beta if c is None else after + (log_beta + c)
        w = jnp.exp2(e)
        if mask is not None:
            w = jnp.where(mask, w, 0.0)
        return w.astype(BF16), after[:, 0:1] + log_stay[:, 0:1]

    pairs = [(i, j) for i in range(nq) for j in ((i, i - 1) if i > 0 else (i,))]
    carry = {}
    acc = {}

    def finish(i, j, lb, ls):
        v = v_ref[j * tq:(j + 1) * tq, :]
        if j == i:
            w, carry[i] = weights(lb, ls, None, causal)
            acc[i] = jnp.dot(w, v, preferred_element_type=F32)
        else:
            w, rs = weights(lb, ls, carry[i], None)
            acc[i] = acc[i] + jnp.dot(w, v, preferred_element_type=F32)
            carry[i] = carry[i] + rs
        if j == max(i - 1, 0):
            o_ref[i * tq:(i + 1) * tq, :] = acc.pop(i)
            c = carry.pop(i)
            if i > 1:
                c_ref[i] = c

    pending = []
    for i, j in pairs:
        pending.append((i, j) + scores(q_ref[i * tq:(i + 1) * tq, :], j * tq))
        if len(pending) > SCORES_AHEAD:
            finish(*pending.pop(0))
    for p in pending:
        finish(*p)

    if nq > 2:
        @pl.when(jnp.max(c_ref[2:nq]) > EXP2_UNDERFLOW)
        def _():
            for i in range(2, nq):
                lo_row = i * tq
                qb = q_ref[lo_row:lo_row + tq, :]

                def cond(state):
                    j, alive, _ = state
                    return jnp.logical_and(j >= 0, alive)

                def body(state):
                    j, _, c = state
                    start = pl.multiple_of(j * tq, tq)
                    lb, ls = scores(qb, start)
                    w, rs = weights(lb, ls, c, None)
                    o_ref[lo_row:lo_row + tq, :] += jnp.dot(
                        w, v_ref[pl.ds(start, tq), :], preferred_element_type=F32)
                    c = c + rs
                    return j - 1, jnp.max(c) > EXP2_UNDERFLOW, c

                c0 = c_ref[i]
                lax.while_loop(cond, body, (i - 2, jnp.max(c0) > EXP2_UNDERFLOW, c0))


def _attention(proj, w_mlp_in, w_out, batch, seq, tq=256):
    t = proj.shape[0]
    nq = seq // tq
    r = lax.broadcasted_iota(jnp.int32, (tq, tq), 0)
    s = lax.broadcasted_iota(jnp.int32, (tq, tq), 1)
    u = (r > s).astype(BF16)
    h = ATTN_HEADS
    steps = batch * h
    d, f = w_mlp_in.shape
    w1_chunk = pl.BlockSpec((d, f // steps), lambda b, hh: (0, b * h + hh))
    wo_chunk = pl.BlockSpec((w_out.shape[0] // steps, w_out.shape[1]), lambda b, hh: (b * h + hh, 0))
    head = lambda off: pl.BlockSpec((seq, HEAD_DIM), lambda b, hh: (b, off + hh))
    return pl.pallas_call(
        functools.partial(_attn_kernel, tq=tq, nq=nq),
        grid=(batch, h),
        in_specs=[head(0), head(h), head(2 * h), pl.BlockSpec((tq, tq), lambda b, hh: (0, 0)),
                  w1_chunk, wo_chunk],
        out_specs=[head(0), w1_chunk, wo_chunk],
        out_shape=[jax.ShapeDtypeStruct((t, h * HEAD_DIM), F32),
                   jax.ShapeDtypeStruct(w_mlp_in.shape, BF16),
                   jax.ShapeDtypeStruct(w_out.shape, BF16)],
        scratch_shapes=[pltpu.VMEM((nq, tq, 1), F32)],
        compiler_params=pltpu.CompilerParams(
            dimension_semantics=("arbitrary", "arbitrary"), vmem_limit_bytes=VMEM_LIMIT),
        name="attn",
    )(proj, proj, proj, u, w_mlp_in, w_out)


def _mix_kernel(xr_ref, xg_ref, oa_ref, x_ref, mod_ref, wc_ref, bc_ref, wg_ref, ba_ref, bx_ref,
                lam_ref, ga_ref, gl_ref, wo_ref, w2f_ref, o_ref, w2b_ref,
                ext_ref, hc_ref, ol_ref, nl_ref, *, ts, ns):
    g = pl.program_id(0)
    hist = 8
    half = oa_ref.shape[1]

    w2b_ref[...] = w2f_ref[...].astype(BF16)

    @pl.when(g % ns == 0)
    def _():
        ext_ref[0:hist, :] = jnp.zeros((hist, ext_ref.shape[1]), F32)
        hc_ref[...] = jnp.zeros_like(hc_ref)

    @pl.when(g == 0)
    def _():
        nl_ref[...] = jnp.zeros_like(nl_ref)

    oa = oa_ref[...]
    na = (oa * _rms_scale(oa) * ga_ref[...]).astype(BF16)
    mixed = jnp.concatenate([na, nl_ref[(g + 1) % 2]], axis=1)
    ncol = o_ref.shape[1] // LRU_HEADS

    def project(h):
        cs = slice(h * ncol, (h + 1) * ncol)
        y = jnp.dot(mixed, wo_ref[:, cs], preferred_element_type=F32)
        o_ref[:, cs] = x_ref[:, cs] + mod_ref[0, 2:3, cs] * y

    ext_ref[hist:hist + ts, :] = xr_ref[...].astype(F32)
    nv = ts // 8
    sub = lax.broadcasted_iota(jnp.int32, (nv, 8, LRU_BLOCK), 1)

    for h in range(LRU_HEADS):
        project(h)
        sl = slice(h * LRU_BLOCK, (h + 1) * LRU_BLOCK)
        wc = wc_ref[:, sl]
        xe = ext_ref[0:hist + ts, sl].reshape(nv + 1, 8, LRU_BLOCK)
        xc = bc_ref[:, sl] + wc[CONV_WIDTH - 1:CONV_WIDTH, :] * xe[1:]
        for d in range(1, CONV_WIDTH):
            rolled = pltpu.roll(xe, d, 1)
            tap = CONV_WIDTH - 1 - d
            xc = xc + wc[tap:tap + 1, :] * jnp.where(sub >= d, rolled[1:], rolled[:-1])
        xc = xc.reshape(ts, LRU_BLOCK)
        gates = jnp.dot(xc.astype(BF16), wg_ref[h], preferred_element_type=F32)
        r = jax.nn.sigmoid(gates[:, :LRU_BLOCK] + ba_ref[:, sl])
        gi = jax.nn.sigmoid(gates[:, LRU_BLOCK:] + bx_ref[:, sl])
        log_a = (-LRU_C) * r * _softplus(-lam_ref[:, sl])
        a = jnp.exp(log_a)
        one_minus_a2 = -jnp.tanh(log_a) * (1.0 + a * a)
        b = jnp.exp2(jnp.log(one_minus_a2) * (0.5 * LOG2_E)) * (gi * xc)
        a = a.reshape(nv, 8, LRU_BLOCK)
        b = b.reshape(nv, 8, LRU_BLOCK)
        for d in (1, 2, 4):
            keep = sub >= d
            a_prev = pltpu.roll(a, d, 1)
            b_prev = pltpu.roll(b, d, 1)
            b = b + a * jnp.where(keep, b_prev, 0.0)
            a = a * jnp.where(keep, a_prev, 1.0)
        gate = jax.nn.gelu(xg_ref[:, sl].astype(F32), approximate=True).reshape(nv, 8, LRU_BLOCK)
        carry = hc_ref[0:1, sl]
        for i in range(nv):
            hcur = b[i] + a[i] * carry
            ol_ref[8 * i:8 * i + 8, sl] = hcur * gate[i]
            carry = hcur[7:8, :]
        hc_ref[0:1, sl] = carry

    ext_ref[0:hist, :] = ext_ref[ts:ts + hist, :]
    ol = ol_ref[...]
    nl_ref[g % 2] = (ol * _rms_scale(ol) * gl_ref[...]).astype(BF16)


def _mix(proj, o_attn, x2, mod, w_conv, b_conv, w_gates, b_a, b_x, lam, g_a, g_l, w_out, w_mlp_out,
         seq, ts=256):
    t, d = x2.shape
    c = LRU_HEADS * LRU_BLOCK
    ns = seq // ts
    nblk = t // ts
    cur = lambda g: jnp.minimum(g, nblk - 1)
    prev = lambda g: jnp.maximum(g - 1, 0)
    vec = pl.BlockSpec((1, c), lambda g: (0, 0))
    xr_col = 3 * ATTN_HEADS * HEAD_DIM // c
    f = w_mlp_out.shape[0]
    w2_chunk = pl.BlockSpec((f // nblk, d), lambda g: (cur(g), 0))
    return pl.pallas_call(
        functools.partial(_mix_kernel, ts=ts, ns=ns),
        grid=(nblk + 1,),
        in_specs=[
            pl.BlockSpec((ts, c), lambda g: (cur(g), xr_col)),
            pl.BlockSpec((ts, c), lambda g: (cur(g), xr_col + 1)),
            pl.BlockSpec((ts, c), lambda g: (prev(g), 0)),
            pl.BlockSpec((ts, d), lambda g: (prev(g), 0)),
            pl.BlockSpec((1, N_MOD, d), lambda g: (prev(g) // ns, 0, 0)),
            pl.BlockSpec((CONV_WIDTH, c), lambda g: (0, 0)),
            vec,
            pl.BlockSpec((LRU_HEADS, LRU_BLOCK, 2 * LRU_BLOCK), lambda g: (0, 0, 0)),
            vec, vec, vec, vec, vec,
            pl.BlockSpec((2 * c, d), lambda g: (0, 0)),
            w2_chunk,
        ],
        out_specs=[pl.BlockSpec((ts, d), lambda g: (prev(g), 0)), w2_chunk],
        out_shape=[jax.ShapeDtypeStruct((t, d), F32), jax.ShapeDtypeStruct(w_mlp_out.shape, BF16)],
        scratch_shapes=[pltpu.VMEM((ts + 8, c), F32), pltpu.VMEM((8, c), F32),
                        pltpu.VMEM((ts, c), F32), pltpu.VMEM((2, ts, c), BF16)],
        compiler_params=pltpu.CompilerParams(
            dimension_semantics=("arbitrary",), vmem_limit_bytes=VMEM_LIMIT),
        name="mix",
    )(proj, proj, o_attn, x2, mod, w_conv, b_conv, w_gates, b_a, b_x, lam, g_a, g_l, w_out, w_mlp_out)


def _mlp_kernel(h0_ref, xc_ref, modn_ref, g_ref, x_ref, mod_ref, w1_ref, w2_ref, gf_ref, o_ref,
                ha_ref, hb_ref):
    j = pl.program_id(1)

    @pl.when(j == 0)
    def _():
        o_ref[...] = jnp.zeros_like(o_ref)

    def consume(h_ref):
        hid = jnp.dot(h_ref[...], w1_ref[...], preferred_element_type=F32)
        hid = jnp.square(jnp.maximum(hid, 0.0)).astype(BF16)
        o_ref[...] += jnp.dot(hid, w2_ref[...], preferred_element_type=F32)

    _with_staged_tiles(h0_ref, xc_ref, modn_ref, g_ref, ha_ref, hb_ref, 3, consume)

    @pl.when(j == pl.num_programs(1) - 1)
    def _():
        x2 = x_ref[...] + mod_ref[0, 5:6, :] * o_ref[...]
        o_ref[...] = x2 * _rms_scale(x2) * gf_ref[...]


def _mlp(x1, mod, g, w1, w2, g_final, seq, tm=512, tf=1024):
    t, d = x1.shape
    f = w1.shape[1]
    nt, nj = t // tm, f // tf
    h0 = _adaln_first_tile(x1, mod, g, 3, tm)
    return pl.pallas_call(
        _mlp_kernel,
        grid=(nt, nj),
        in_specs=[pl.BlockSpec((tm, d), lambda i, j: (0, 0))] + _next_tile_specs(tm, nj, nt, d, seq) + [
            pl.BlockSpec((1, d), lambda i, j: (0, 0)),
            pl.BlockSpec((tm, d), lambda i, j: (i, 0)),
            pl.BlockSpec((1, N_MOD, d), lambda i, j: (i * tm // seq, 0, 0)),
            pl.BlockSpec((d, tf), lambda i, j: (0, j)),
            pl.BlockSpec((tf, d), lambda i, j: (j, 0)),
            pl.BlockSpec((1, d), lambda i, j: (0, 0)),
        ],
        out_specs=pl.BlockSpec((tm, d), lambda i, j: (i, 0)),
        out_shape=jax.ShapeDtypeStruct((t, d), F32),
        scratch_shapes=[pltpu.VMEM((tm, d), BF16), pltpu.VMEM((tm, d), BF16)],
        compiler_params=pltpu.CompilerParams(
            dimension_semantics=("arbitrary", "arbitrary"), vmem_limit_bytes=VMEM_LIMIT),
        name="mlp",
    )(h0, x1, mod, g, x1, mod, w1, w2, g_final)


def kernel(x, c, w_ada, b_ada, g_norm_mix, w_in, w_conv, b_conv, w_rg_a, b_rg_a, w_rg_x, b_rg_x,
           lru_lambda, g_attn_out, g_lru_out, w_out, g_norm_mlp, w_mlp_in, w_mlp_out, g_norm_final):
    batch, seq, d = x.shape
    assert w_ada.shape[0] == 1, "single layer: the final norm is fused into the MLP kernel"
    xt = x.reshape(batch * seq, d)
    c_pad = jnp.pad(c.astype(F32), ((0, 8 - batch), (0, 0)))
    mod = _ada(c_pad, w_ada[0], b_ada)[:batch].reshape(batch, N_MOD, d)
    proj = _inproj(xt, mod, g_norm_mix, w_in[0].astype(BF16), seq)
    o_attn, w1, wo = _attention(proj, w_mlp_in[0], w_out[0], batch, seq)
    w_gates = jnp.concatenate([w_rg_a[0], w_rg_x[0]], axis=-1).astype(BF16)
    x1, w2 = _mix(proj, o_attn, xt, mod, w_conv[0], b_conv, w_gates, b_rg_a, b_rg_x, lru_lambda,
                  g_attn_out, g_lru_out, wo, w_mlp_out[0], seq)
    out = _mlp(x1, mod, g_norm_mlp, w1, w2, g_norm_final[None, :], seq)
    return out.reshape(batch, seq, d)
```

```python
import functools

import jax
import jax.numpy as jnp
from jax import lax
from jax.experimental import pallas as pl
from jax.experimental.pallas import tpu as pltpu

F32 = jnp.float32
BF16 = jnp.bfloat16

ATTN_HEADS = 8
HEAD_DIM = 128
LRU_HEADS = 8
LRU_BLOCK = 128
CONV_WIDTH = 4
LRU_C = 8.0
N_MOD = 6
EPS = 1e-6

LOG2_E = 1.4426950408889634
EXP2_UNDERFLOW = -151.0
MASKED_LOGIT = 1e30

VMEM_LIMIT = 56 * 1024 * 1024
SCORES_AHEAD = 2


def _softplus(x):
    return jnp.maximum(x, 0.0) + jnp.log1p(jnp.exp(-jnp.abs(x)))


def _rms_scale(x):
    return lax.rsqrt(jnp.mean(x * x, axis=-1, keepdims=True) + EPS)


def _ada_kernel(c_ref, w_ref, b_ref, winf_ref, o_ref, winb_ref):
    winb_ref[...] = winf_ref[...].astype(BF16)
    c = c_ref[...]
    c_act = c * jax.nn.sigmoid(c)
    o_ref[...] = jnp.dot(c_act.astype(BF16), w_ref[...].astype(BF16),
                         preferred_element_type=F32) + b_ref[...]


def _ada(c_pad, w_ada, b_ada, w_in, tn=768):
    m, d = c_pad.shape
    n = w_ada.shape[1]
    steps = n // tn
    win_chunk = pl.BlockSpec((w_in.shape[0] // steps, w_in.shape[1]), lambda j: (j, 0))
    return pl.pallas_call(
        _ada_kernel,
        grid=(steps,),
        in_specs=[
            pl.BlockSpec((m, d), lambda j: (0, 0)),
            pl.BlockSpec((d, tn), lambda j: (0, j)),
            pl.BlockSpec((1, tn), lambda j: (0, j)),
            win_chunk,
        ],
        out_specs=[pl.BlockSpec((m, tn), lambda j: (0, j)), win_chunk],
        out_shape=[jax.ShapeDtypeStruct((m, n), F32), jax.ShapeDtypeStruct(w_in.shape, BF16)],
        compiler_params=pltpu.CompilerParams(
            dimension_semantics=("arbitrary",), vmem_limit_bytes=VMEM_LIMIT),
        name="ada",
    )(c_pad, w_ada, b_ada, w_in)


def _adaln(x, g, shift, scale):
    return ((x * _rms_scale(x) * g) * (1.0 + scale) + shift).astype(BF16)


def _with_staged_tiles(x0_ref, mod0_ref, xc_ref, mod_ref, g_ref, ha_ref, hb_ref, row, consume):
    i, j = pl.program_id(0), pl.program_id(1)

    def adaln(x, m_ref):
        return _adaln(x, g_ref[...], m_ref[0, row:row + 1, :], m_ref[0, row + 1:row + 2, :])

    @pl.when(jnp.logical_and(i == 0, j == 0))
    def _():
        ha_ref[...] = adaln(x0_ref[...], mod0_ref)

    rc = xc_ref.shape[0]
    rows = pl.ds(pl.multiple_of(j * rc, rc), rc)

    def step(cur_ref, nxt_ref):
        consume(cur_ref)
        nxt_ref[rows, :] = adaln(xc_ref[...], mod_ref)

    @pl.when(i % 2 == 0)
    def _():
        step(ha_ref, hb_ref)

    @pl.when(i % 2 == 1)
    def _():
        step(hb_ref, ha_ref)


def _staged_tile_specs(tm, nj, nt, d, seq):
    nxt = lambda i: jnp.minimum(i + 1, nt - 1)
    once = pl.Buffered(1)
    return [pl.BlockSpec((tm, d), lambda i, j: (0, 0), pipeline_mode=once),
            pl.BlockSpec((1, N_MOD, d), lambda i, j: (0, 0, 0), pipeline_mode=once),
            pl.BlockSpec((tm // nj, d), lambda i, j: (nxt(i) * nj + j, 0)),
            pl.BlockSpec((1, N_MOD, d), lambda i, j: (nxt(i) * tm // seq, 0, 0))]


def _inproj_kernel(x0_ref, mod0_ref, xc_ref, mod_ref, g_ref, w_ref, cs_ref, o_ref, ha_ref, hb_ref):
    def consume(h_ref):
        res = jnp.dot(h_ref[...], w_ref[...], preferred_element_type=F32)
        o_ref[...] = (res * cs_ref[...]).astype(BF16)

    _with_staged_tiles(x0_ref, mod0_ref, xc_ref, mod_ref, g_ref, ha_ref, hb_ref, 0, consume)


def _inproj(x2, mod, g, w_in, seq, tm=1024, tn=1280):
    t, d = x2.shape
    n = w_in.shape[1]
    nt, nj = t // tm, n // tn
    col = lax.broadcasted_iota(jnp.int32, (1, n), 1)
    colscale = jnp.where(col < ATTN_HEADS * HEAD_DIM, HEAD_DIM ** -0.5 * LOG2_E, 1.0).astype(F32)
    return pl.pallas_call(
        _inproj_kernel,
        grid=(nt, nj),
        in_specs=_staged_tile_specs(tm, nj, nt, d, seq) + [
            pl.BlockSpec((1, d), lambda i, j: (0, 0)),
            pl.BlockSpec((d, tn), lambda i, j: (0, j)),
            pl.BlockSpec((1, tn), lambda i, j: (0, j)),
        ],
        out_specs=pl.BlockSpec((tm, tn), lambda i, j: (i, j)),
        out_shape=jax.ShapeDtypeStruct((t, n), BF16),
        scratch_shapes=[pltpu.VMEM((tm, d), BF16), pltpu.VMEM((tm, d), BF16)],
        compiler_params=pltpu.CompilerParams(
            dimension_semantics=("arbitrary", "arbitrary"), vmem_limit_bytes=VMEM_LIMIT),
        name="inproj",
    )(x2, mod, x2, mod, g, w_in, colscale)


def _attn_kernel(q_ref, k_ref, v_ref, u_ref, w1f_ref, wof_ref, o_ref, w1b_ref, wob_ref, c_ref,
                 *, tq, nq):
    w1b_ref[...] = w1f_ref[...].astype(BF16)
    wob_ref[...] = wof_ref[...].astype(BF16)

    u = u_ref[...]
    rows = lax.broadcasted_iota(jnp.int32, (tq, tq), 0)
    cols = lax.broadcasted_iota(jnp.int32, (tq, tq), 1)
    causal = cols < rows

    def scores(qb, start, mask):
        k = k_ref[pl.ds(start, tq), :]
        z = lax.dot_general(qb, k, (((1,), (1,)), ((), ())), preferred_element_type=F32)
        if mask is not None:
            z = jnp.where(mask, z, -MASKED_LOGIT)
        t = jnp.log2(1.0 + jnp.exp2(-jnp.abs(z)))
        log_beta = jnp.minimum(z, 0.0) - t
        return log_beta, log_beta - z

    def weights(log_beta, log_stay, c):
        after = jnp.dot(log_stay.astype(BF16), u, preferred_element_type=F32)
        e = after + log_beta if c is None else after + (log_beta + c)
        return jnp.exp2(e).astype(BF16), after[:, 0:1] + log_stay[:, 0:1]

    pairs = [(i, j) for i in range(nq) for j in ((i, i - 1) if i > 0 else (i,))]
    carry = {}
    acc = {}

    def finish(i, j, lb, ls):
        v = v_ref[j * tq:(j + 1) * tq, :]
        if j == i:
            w, carry[i] = weights(lb, ls, None)
            acc[i] = jnp.dot(w, v, preferred_element_type=F32)
        else:
            w, rs = weights(lb, ls, carry[i])
            acc[i] = acc[i] + jnp.dot(w, v, preferred_element_type=F32)
            carry[i] = carry[i] + rs
        if j == max(i - 1, 0):
            o_ref[i * tq:(i + 1) * tq, :] = acc.pop(i)
            c = carry.pop(i)
            if i > 1:
                c_ref[i] = c

    pending = []
    for i, j in pairs:
        pending.append((i, j) + scores(q_ref[i * tq:(i + 1) * tq, :], j * tq,
                                       causal if j == i else None))
        if len(pending) > SCORES_AHEAD:
            finish(*pending.pop(0))
    for p in pending:
        finish(*p)

    if nq > 2:
        @pl.when(jnp.max(c_ref[2:nq]) > EXP2_UNDERFLOW)
        def _():
            for i in range(2, nq):
                lo_row = i * tq
                qb = q_ref[lo_row:lo_row + tq, :]

                def cond(state):
                    j, alive, _ = state
                    return jnp.logical_and(j >= 0, alive)

                def body(state):
                    j, _, c = state
                    start = pl.multiple_of(j * tq, tq)
                    lb, ls = scores(qb, start, None)
                    w, rs = weights(lb, ls, c)
                    o_ref[lo_row:lo_row + tq, :] += jnp.dot(
                        w, v_ref[pl.ds(start, tq), :], preferred_element_type=F32)
                    c = c + rs
                    return j - 1, jnp.max(c) > EXP2_UNDERFLOW, c

                c0 = c_ref[i]
                lax.while_loop(cond, body, (i - 2, jnp.max(c0) > EXP2_UNDERFLOW, c0))


def _attention(proj, w_mlp_in, w_out, batch, seq, tq=256):
    t = proj.shape[0]
    nq = seq // tq
    r = lax.broadcasted_iota(jnp.int32, (tq, tq), 0)
    s = lax.broadcasted_iota(jnp.int32, (tq, tq), 1)
    u = (r > s).astype(BF16)
    h = ATTN_HEADS
    steps = batch * h
    d, f = w_mlp_in.shape
    w1_chunk = pl.BlockSpec((d, f // steps), lambda b, hh: (0, b * h + hh))
    wo_chunk = pl.BlockSpec((w_out.shape[0] // steps, w_out.shape[1]), lambda b, hh: (b * h + hh, 0))
    head = lambda off: pl.BlockSpec((seq, HEAD_DIM), lambda b, hh: (b, off + hh))
    return pl.pallas_call(
        functools.partial(_attn_kernel, tq=tq, nq=nq),
        grid=(batch, h),
        in_specs=[head(0), head(h), head(2 * h), pl.BlockSpec((tq, tq), lambda b, hh: (0, 0)),
                  w1_chunk, wo_chunk],
        out_specs=[head(0), w1_chunk, wo_chunk],
        out_shape=[jax.ShapeDtypeStruct((t, h * HEAD_DIM), F32),
                   jax.ShapeDtypeStruct(w_mlp_in.shape, BF16),
                   jax.ShapeDtypeStruct(w_out.shape, BF16)],
        scratch_shapes=[pltpu.VMEM((nq, tq, 1), F32)],
        compiler_params=pltpu.CompilerParams(
            dimension_semantics=("arbitrary", "arbitrary"), vmem_limit_bytes=VMEM_LIMIT),
        name="attn",
    )(proj, proj, proj, u, w_mlp_in, w_out)


def _mix_kernel(xr_ref, xg_ref, oa_ref, x_ref, mod_ref, wc_ref, bc_ref, wg_ref, ba_ref, bx_ref,
                lam_ref, ga_ref, gl_ref, wo_ref, w2f_ref, o_ref, w2b_ref,
                ext_ref, hc_ref, ol_ref, nl_ref, *, ts, ns):
    g = pl.program_id(0)
    hist = 8
    half = oa_ref.shape[1]

    w2b_ref[...] = w2f_ref[...].astype(BF16)

    @pl.when(g % ns == 0)
    def _():
        ext_ref[0:hist, :] = jnp.zeros((hist, ext_ref.shape[1]), F32)
        hc_ref[...] = jnp.zeros_like(hc_ref)

    @pl.when(g == 0)
    def _():
        nl_ref[...] = jnp.zeros_like(nl_ref)

    oa = oa_ref[...]
    na = (oa * _rms_scale(oa) * ga_ref[...]).astype(BF16)
    mixed = jnp.concatenate([na, nl_ref[(g + 1) % 2]], axis=1)
    ncol = o_ref.shape[1] // LRU_HEADS

    def project(h):
        cs = slice(h * ncol, (h + 1) * ncol)
        y = jnp.dot(mixed, wo_ref[:, cs], preferred_element_type=F32)
        o_ref[:, cs] = x_ref[:, cs] + mod_ref[0, 2:3, cs] * y

    ext_ref[hist:hist + ts, :] = xr_ref[...].astype(F32)
    nv = ts // 8
    sub = lax.broadcasted_iota(jnp.int32, (nv, 8, LRU_BLOCK), 1)

    def conv_and_gates(h):
        sl = slice(h * LRU_BLOCK, (h + 1) * LRU_BLOCK)
        wc = wc_ref[:, sl]
        xe = ext_ref[0:hist + ts, sl].reshape(nv + 1, 8, LRU_BLOCK)
        xc = bc_ref[:, sl] + wc[CONV_WIDTH - 1:CONV_WIDTH, :] * xe[1:]
        for d in range(1, CONV_WIDTH):
            rolled = pltpu.roll(xe, d, 1)
            tap = CONV_WIDTH - 1 - d
            xc = xc + wc[tap:tap + 1, :] * jnp.where(sub >= d, rolled[1:], rolled[:-1])
        xc = xc.reshape(ts, LRU_BLOCK)
        return xc, jnp.dot(xc.astype(BF16), wg_ref[h], preferred_element_type=F32)

    def recurrence(h, xc, gates):
        sl = slice(h * LRU_BLOCK, (h + 1) * LRU_BLOCK)
        r = jax.nn.sigmoid(gates[:, :LRU_BLOCK] + ba_ref[:, sl])
        gi = jax.nn.sigmoid(gates[:, LRU_BLOCK:] + bx_ref[:, sl])
        log_a = (-LRU_C) * r * _softplus(-lam_ref[:, sl])
        a = jnp.exp(log_a)
        one_minus_a2 = -jnp.tanh(log_a) * (1.0 + a * a)
        b = jnp.exp2(jnp.log(one_minus_a2) * (0.5 * LOG2_E)) * (gi * xc)
        a = a.reshape(nv, 8, LRU_BLOCK)
        b = b.reshape(nv, 8, LRU_BLOCK)
        for d in (1, 2, 4):
            keep = sub >= d
            a_prev = pltpu.roll(a, d, 1)
            b_prev = pltpu.roll(b, d, 1)
            b = b + a * jnp.where(keep, b_prev, 0.0)
            a = a * jnp.where(keep, a_prev, 1.0)
        gate = jax.nn.gelu(xg_ref[:, sl].astype(F32), approximate=True).reshape(nv, 8, LRU_BLOCK)
        carry = hc_ref[0:1, sl]
        for i in range(nv):
            hcur = b[i] + a[i] * carry
            ol_ref[8 * i:8 * i + 8, sl] = hcur * gate[i]
            carry = hcur[7:8, :]
        hc_ref[0:1, sl] = carry

    staged = conv_and_gates(0)
    for h in range(LRU_HEADS):
        ready = staged
        if h + 1 < LRU_HEADS:
            staged = conv_and_gates(h + 1)
        project(h)
        recurrence(h, *ready)

    ext_ref[0:hist, :] = ext_ref[ts:ts + hist, :]
    ol = ol_ref[...]
    nl_ref[g % 2] = (ol * _rms_scale(ol) * gl_ref[...]).astype(BF16)


def _mix(proj, o_attn, x2, mod, w_conv, b_conv, w_gates, b_a, b_x, lam, g_a, g_l, w_out, w_mlp_out,
         seq, ts=256):
    t, d = x2.shape
    c = LRU_HEADS * LRU_BLOCK
    ns = seq // ts
    nblk = t // ts
    cur = lambda g: jnp.minimum(g, nblk - 1)
    prev = lambda g: jnp.maximum(g - 1, 0)
    vec = pl.BlockSpec((1, c), lambda g: (0, 0))
    xr_col = 3 * ATTN_HEADS * HEAD_DIM // c
    f = w_mlp_out.shape[0]
    w2_chunk = pl.BlockSpec((f // nblk, d), lambda g: (cur(g), 0))
    return pl.pallas_call(
        functools.partial(_mix_kernel, ts=ts, ns=ns),
        grid=(nblk + 1,),
        in_specs=[
            pl.BlockSpec((ts, c), lambda g: (cur(g), xr_col)),
            pl.BlockSpec((ts, c), lambda g: (cur(g), xr_col + 1)),
            pl.BlockSpec((ts, c), lambda g: (prev(g), 0)),
            pl.BlockSpec((ts, d), lambda g: (prev(g), 0)),
            pl.BlockSpec((1, N_MOD, d), lambda g: (prev(g) // ns, 0, 0)),
            pl.BlockSpec((CONV_WIDTH, c), lambda g: (0, 0)),
            vec,
            pl.BlockSpec((LRU_HEADS, LRU_BLOCK, 2 * LRU_BLOCK), lambda g: (0, 0, 0)),
            vec, vec, vec, vec, vec,
            pl.BlockSpec((2 * c, d), lambda g: (0, 0)),
            w2_chunk,
        ],
        out_specs=[pl.BlockSpec((ts, d), lambda g: (prev(g), 0)), w2_chunk],
        out_shape=[jax.ShapeDtypeStruct((t, d), F32), jax.ShapeDtypeStruct(w_mlp_out.shape, BF16)],
        scratch_shapes=[pltpu.VMEM((ts + 8, c), F32), pltpu.VMEM((8, c), F32),
                        pltpu.VMEM((ts, c), F32), pltpu.VMEM((2, ts, c), BF16)],
        compiler_params=pltpu.CompilerParams(
            dimension_semantics=("arbitrary",), vmem_limit_bytes=VMEM_LIMIT),
        name="mix",
    )(proj, proj, o_attn, x2, mod, w_conv, b_conv, w_gates, b_a, b_x, lam, g_a, g_l, w_out, w_mlp_out)


def _mlp_kernel(x0_ref, mod0_ref, xc_ref, modn_ref, g_ref, x_ref, mod_ref, w1_ref, w2_ref, gf_ref,
                o_ref, ha_ref, hb_ref):
    j = pl.program_id(1)

    @pl.when(j == 0)
    def _():
        o_ref[...] = jnp.zeros_like(o_ref)

    def consume(h_ref):
        hid = jnp.dot(h_ref[...], w1_ref[...], preferred_element_type=F32)
        hid = jnp.square(jnp.maximum(hid, 0.0)).astype(BF16)
        o_ref[...] += jnp.dot(hid, w2_ref[...], preferred_element_type=F32)

    _with_staged_tiles(x0_ref, mod0_ref, xc_ref, modn_ref, g_ref, ha_ref, hb_ref, 3, consume)

    @pl.when(j == pl.num_programs(1) - 1)
    def _():
        x2 = x_ref[...] + mod_ref[0, 5:6, :] * o_ref[...]
        o_ref[...] = x2 * _rms_scale(x2) * gf_ref[...]


def _mlp(x1, mod, g, w1, w2, g_final, seq, tm=512, tf=1024):
    t, d = x1.shape
    f = w1.shape[1]
    nt, nj = t // tm, f // tf
    return pl.pallas_call(
        _mlp_kernel,
        grid=(nt, nj),
        in_specs=_staged_tile_specs(tm, nj, nt, d, seq) + [
            pl.BlockSpec((1, d), lambda i, j: (0, 0)),
            pl.BlockSpec((tm, d), lambda i, j: (i, 0)),
            pl.BlockSpec((1, N_MOD, d), lambda i, j: (i * tm // seq, 0, 0)),
            pl.BlockSpec((d, tf), lambda i, j: (0, j)),
            pl.BlockSpec((tf, d), lambda i, j: (j, 0)),
            pl.BlockSpec((1, d), lambda i, j: (0, 0)),
        ],
        out_specs=pl.BlockSpec((tm, d), lambda i, j: (i, 0)),
        out_shape=jax.ShapeDtypeStruct((t, d), F32),
        scratch_shapes=[pltpu.VMEM((tm, d), BF16), pltpu.VMEM((tm, d), BF16)],
        compiler_params=pltpu.CompilerParams(
            dimension_semantics=("arbitrary", "arbitrary"), vmem_limit_bytes=VMEM_LIMIT),
        name="mlp",
    )(x1, mod, x1, mod, g, x1, mod, w1, w2, g_final)


def kernel(x, c, w_ada, b_ada, g_norm_mix, w_in, w_conv, b_conv, w_rg_a, b_rg_a, w_rg_x, b_rg_x,
           lru_lambda, g_attn_out, g_lru_out, w_out, g_norm_mlp, w_mlp_in, w_mlp_out, g_norm_final):
    batch, seq, d = x.shape
    assert w_ada.shape[0] == 1, "single layer: the final norm is fused into the MLP kernel"
    xt = x.reshape(batch * seq, d)
    c_pad = jnp.pad(c.astype(F32), ((0, 8 - batch), (0, 0)))
    mod, w_in_bf16 = _ada(c_pad, w_ada[0], b_ada, w_in[0])
    mod = mod[:batch].reshape(batch, N_MOD, d)
    proj = _inproj(xt, mod, g_norm_mix, w_in_bf16, seq)
    o_attn, w1, wo = _attention(proj, w_mlp_in[0], w_out[0], batch, seq)
    w_gates = jnp.concatenate([w_rg_a[0], w_rg_x[0]], axis=-1).astype(BF16)
    x1, w2 = _mix(proj, o_attn, xt, mod, w_conv[0], b_conv, w_gates, b_rg_a, b_rg_x, lru_lambda,
                  g_attn_out, g_lru_out, wo, w_mlp_out[0], seq)
    out = _mlp(x1, mod, g_norm_mlp, w1, w2, g_norm_final[None, :], seq)
    return out.reshape(batch, seq, d)
```

```python
import functools

import jax
import jax.numpy as jnp
from jax import lax
from jax.experimental import pallas as pl
from jax.experimental.pallas import tpu as pltpu

F32 = jnp.float32
BF16 = jnp.bfloat16

ATTN_HEADS = 8
HEAD_DIM = 128
LRU_HEADS = 8
LRU_BLOCK = 128
CONV_WIDTH = 4
LRU_C = 8.0
N_MOD = 6
EPS = 1e-6

LOG2_E = 1.4426950408889634
EXP2_UNDERFLOW = -151.0
MASKED_LOGIT = 1e30

VMEM_LIMIT = 56 * 1024 * 1024
SCORES_AHEAD = 2


def _softplus(x):
    return jnp.maximum(x, 0.0) + jnp.log1p(jnp.exp(-jnp.abs(x)))


def _rms_scale(x):
    return lax.rsqrt(jnp.mean(x * x, axis=-1, keepdims=True) + EPS)


def _ada_kernel(c_ref, w_ref, b_ref, winf_ref, o_ref, winb_ref):
    winb_ref[...] = winf_ref[...].astype(BF16)
    c = c_ref[...]
    c_act = c * jax.nn.sigmoid(c)
    o_ref[...] = jnp.dot(c_act.astype(BF16), w_ref[...].astype(BF16),
                         preferred_element_type=F32) + b_ref[...]


def _ada(c_pad, w_ada, b_ada, w_in, tn=768):
    m, d = c_pad.shape
    n = w_ada.shape[1]
    steps = n // tn
    win_chunk = pl.BlockSpec((w_in.shape[0] // steps, w_in.shape[1]), lambda j: (j, 0))
    return pl.pallas_call(
        _ada_kernel,
        grid=(steps,),
        in_specs=[
            pl.BlockSpec((m, d), lambda j: (0, 0)),
            pl.BlockSpec((d, tn), lambda j: (0, j)),
            pl.BlockSpec((1, tn), lambda j: (0, j)),
            win_chunk,
        ],
        out_specs=[pl.BlockSpec((m, tn), lambda j: (0, j)), win_chunk],
        out_shape=[jax.ShapeDtypeStruct((m, n), F32), jax.ShapeDtypeStruct(w_in.shape, BF16)],
        compiler_params=pltpu.CompilerParams(
            dimension_semantics=("arbitrary",), vmem_limit_bytes=VMEM_LIMIT),
        name="ada",
    )(c_pad, w_ada, b_ada, w_in)


def _adaln(x, g, shift, scale):
    return ((x * _rms_scale(x) * g) * (1.0 + scale) + shift).astype(BF16)


def _inproj_kernel(x_ref, mod_ref, g_ref, w_ref, cs_ref, o_ref, h_ref, *, row_blocks):
    j = pl.program_id(1)
    rb = x_ref.shape[0] // row_blocks

    def project(h):
        return (jnp.dot(h, w_ref[...], preferred_element_type=F32) * cs_ref[...]).astype(BF16)

    @pl.when(j == 0)
    def _():
        for r in range(row_blocks):
            rows = slice(r * rb, (r + 1) * rb)
            h = _adaln(x_ref[rows, :], g_ref[...], mod_ref[0, 0:1, :], mod_ref[0, 1:2, :])
            h_ref[rows, :] = h
            o_ref[rows, :] = project(h)

    @pl.when(j > 0)
    def _():
        o_ref[...] = project(h_ref[...])


def _inproj(x2, mod, g, w_in, seq, tm=1024, tn=1280):
    t, d = x2.shape
    n = w_in.shape[1]
    nt, nj = t // tm, n // tn
    col = lax.broadcasted_iota(jnp.int32, (1, n), 1)
    colscale = jnp.where(col < ATTN_HEADS * HEAD_DIM, HEAD_DIM ** -0.5 * LOG2_E, 1.0).astype(F32)
    return pl.pallas_call(
        functools.partial(_inproj_kernel, row_blocks=4),
        grid=(nt, nj),
        in_specs=[
            pl.BlockSpec((tm, d), lambda i, j: (i, 0)),
            pl.BlockSpec((1, N_MOD, d), lambda i, j: (i * tm // seq, 0, 0)),
            pl.BlockSpec((1, d), lambda i, j: (0, 0)),
            pl.BlockSpec((d, tn), lambda i, j: (0, j)),
            pl.BlockSpec((1, tn), lambda i, j: (0, j)),
        ],
        out_specs=pl.BlockSpec((tm, tn), lambda i, j: (i, j)),
        out_shape=jax.ShapeDtypeStruct((t, n), BF16),
        scratch_shapes=[pltpu.VMEM((tm, d), BF16)],
        compiler_params=pltpu.CompilerParams(
            dimension_semantics=("arbitrary", "arbitrary"), vmem_limit_bytes=VMEM_LIMIT),
        name="inproj",
    )(x2, mod, g, w_in, colscale)


def _attn_kernel(q_ref, k_ref, v_ref, u_ref, w1f_ref, wof_ref, o_ref, w1b_ref, wob_ref, c_ref,
                 *, tq, nq):
    w1b_ref[...] = w1f_ref[...].astype(BF16)
    wob_ref[...] = wof_ref[...].astype(BF16)

    u = u_ref[...]
    rows = lax.broadcasted_iota(jnp.int32, (tq, tq), 0)
    cols = lax.broadcasted_iota(jnp.int32, (tq, tq), 1)
    causal = cols < rows

    def scores(qb, start, mask):
        k = k_ref[pl.ds(start, tq), :]
        z = lax.dot_general(qb, k, (((1,), (1,)), ((), ())), preferred_element_type=F32)
        if mask is not None:
            z = jnp.where(mask, z, -MASKED_LOGIT)
        t = jnp.log2(1.0 + jnp.exp2(-jnp.abs(z)))
        log_beta = jnp.minimum(z, 0.0) - t
        return log_beta, log_beta - z

    def weights(log_beta, log_stay, c):
        after = jnp.dot(log_stay.astype(BF16), u, preferred_element_type=F32)
        e = after + log_beta if c is None else after + (log_beta + c)
        return jnp.exp2(e).astype(BF16), after[:, 0:1] + log_stay[:, 0:1]

    pairs = [(i, j) for i in range(nq) for j in ((i, i - 1) if i > 0 else (i,))]
    carry = {}
    acc = {}

    def finish(i, j, lb, ls):
        v = v_ref[j * tq:(j + 1) * tq, :]
        if j == i:
            w, carry[i] = weights(lb, ls, None)
            acc[i] = jnp.dot(w, v, preferred_element_type=F32)
        else:
            w, rs = weights(lb, ls, carry[i])
            acc[i] = acc[i] + jnp.dot(w, v, preferred_element_type=F32)
            carry[i] = carry[i] + rs
        if j == max(i - 1, 0):
            o_ref[i * tq:(i + 1) * tq, :] = acc.pop(i)
            c = carry.pop(i)
            if i > 1:
                c_ref[i] = c

    pending = []
    for i, j in pairs:
        pending.append((i, j) + scores(q_ref[i * tq:(i + 1) * tq, :], j * tq,
                                       causal if j == i else None))
        if len(pending) > SCORES_AHEAD:
            finish(*pending.pop(0))
    for p in pending:
        finish(*p)

    if nq > 2:
        @pl.when(jnp.max(c_ref[2:nq]) > EXP2_UNDERFLOW)
        def _():
            for i in range(2, nq):
                lo_row = i * tq
                qb = q_ref[lo_row:lo_row + tq, :]

                def cond(state):
                    j, alive, _ = state
                    return jnp.logical_and(j >= 0, alive)

                def body(state):
                    j, _, c = state
                    start = pl.multiple_of(j * tq, tq)
                    lb, ls = scores(qb, start, None)
                    w, rs = weights(lb, ls, c)
                    o_ref[lo_row:lo_row + tq, :] += jnp.dot(
                        w, v_ref[pl.ds(start, tq), :], preferred_element_type=F32)
                    c = c + rs
                    return j - 1, jnp.max(c) > EXP2_UNDERFLOW, c

                c0 = c_ref[i]
                lax.while_loop(cond, body, (i - 2, jnp.max(c0) > EXP2_UNDERFLOW, c0))


def _attention(proj, w_mlp_in, w_out, batch, seq, tq=256):
    t = proj.shape[0]
    nq = seq // tq
    r = lax.broadcasted_iota(jnp.int32, (tq, tq), 0)
    s = lax.broadcasted_iota(jnp.int32, (tq, tq), 1)
    u = (r > s).astype(BF16)
    h = ATTN_HEADS
    steps = batch * h
    d, f = w_mlp_in.shape
    w1_chunk = pl.BlockSpec((d, f // steps), lambda b, hh: (0, b * h + hh))
    wo_chunk = pl.BlockSpec((w_out.shape[0] // steps, w_out.shape[1]), lambda b, hh: (b * h + hh, 0))
    head = lambda off: pl.BlockSpec((seq, HEAD_DIM), lambda b, hh: (b, off + hh))
    return pl.pallas_call(
        functools.partial(_attn_kernel, tq=tq, nq=nq),
        grid=(batch, h),
        in_specs=[head(0), head(h), head(2 * h), pl.BlockSpec((tq, tq), lambda b, hh: (0, 0)),
                  w1_chunk, wo_chunk],
        out_specs=[head(0), w1_chunk, wo_chunk],
        out_shape=[jax.ShapeDtypeStruct((t, h * HEAD_DIM), F32),
                   jax.ShapeDtypeStruct(w_mlp_in.shape, BF16),
                   jax.ShapeDtypeStruct(w_out.shape, BF16)],
        scratch_shapes=[pltpu.VMEM((nq, tq, 1), F32)],
        compiler_params=pltpu.CompilerParams(
            dimension_semantics=("arbitrary", "arbitrary"), vmem_limit_bytes=VMEM_LIMIT),
        name="attn",
    )(proj, proj, proj, u, w_mlp_in, w_out)


def _mix_kernel(xr_ref, xg_ref, oa_ref, x_ref, mod_ref, wc_ref, bc_ref, wg_ref, ba_ref, bx_ref,
                lam_ref, ga_ref, gl_ref, wo_ref, w2f_ref, o_ref, w2b_ref,
                ext_ref, hc_ref, ol_ref, nl_ref, *, ts, ns):
    g = pl.program_id(0)
    hist = 8
    half = oa_ref.shape[1]

    w2b_ref[...] = w2f_ref[...].astype(BF16)

    @pl.when(g % ns == 0)
    def _():
        ext_ref[0:hist, :] = jnp.zeros((hist, ext_ref.shape[1]), F32)
        hc_ref[...] = jnp.zeros_like(hc_ref)

    @pl.when(g == 0)
    def _():
        nl_ref[...] = jnp.zeros_like(nl_ref)

    oa = oa_ref[...]
    na = (oa * _rms_scale(oa) * ga_ref[...]).astype(BF16)
    mixed = jnp.concatenate([na, nl_ref[(g + 1) % 2]], axis=1)
    ncol = o_ref.shape[1] // LRU_HEADS

    def project(h):
        cs = slice(h * ncol, (h + 1) * ncol)
        y = jnp.dot(mixed, wo_ref[:, cs], preferred_element_type=F32)
        o_ref[:, cs] = x_ref[:, cs] + mod_ref[0, 2:3, cs] * y

    ext_ref[hist:hist + ts, :] = xr_ref[...].astype(F32)
    nv = ts // 8
    sub = lax.broadcasted_iota(jnp.int32, (nv, 8, LRU_BLOCK), 1)

    def conv_and_gates(h):
        sl = slice(h * LRU_BLOCK, (h + 1) * LRU_BLOCK)
        wc = wc_ref[:, sl]
        xe = ext_ref[0:hist + ts, sl].reshape(nv + 1, 8, LRU_BLOCK)
        xc = bc_ref[:, sl] + wc[CONV_WIDTH - 1:CONV_WIDTH, :] * xe[1:]
        for d in range(1, CONV_WIDTH):
            rolled = pltpu.roll(xe, d, 1)
            tap = CONV_WIDTH - 1 - d
            xc = xc + wc[tap:tap + 1, :] * jnp.where(sub >= d, rolled[1:], rolled[:-1])
        xc = xc.reshape(ts, LRU_BLOCK)
        return xc, jnp.dot(xc.astype(BF16), wg_ref[h], preferred_element_type=F32)

    def recurrence(h, xc, gates):
        sl = slice(h * LRU_BLOCK, (h + 1) * LRU_BLOCK)
        r = jax.nn.sigmoid(gates[:, :LRU_BLOCK] + ba_ref[:, sl])
        gi = jax.nn.sigmoid(gates[:, LRU_BLOCK:] + bx_ref[:, sl])
        log_a = (-LRU_C) * r * _softplus(-lam_ref[:, sl])
        a = jnp.exp(log_a)
        one_minus_a2 = -jnp.tanh(log_a) * (1.0 + a * a)
        b = jnp.exp2(jnp.log(one_minus_a2) * (0.5 * LOG2_E)) * (gi * xc)
        a = a.reshape(nv, 8, LRU_BLOCK)
        b = b.reshape(nv, 8, LRU_BLOCK)
        for d in (1, 2, 4):
            keep = sub >= d
            a_prev = pltpu.roll(a, d, 1)
            b_prev = pltpu.roll(b, d, 1)
            b = b + a * jnp.where(keep, b_prev, 0.0)
            a = a * jnp.where(keep, a_prev, 1.0)
        gate = jax.nn.gelu(xg_ref[:, sl].astype(F32), approximate=True).reshape(nv, 8, LRU_BLOCK)
        carry = hc_ref[0:1, sl]
        for i in range(nv):
            hcur = b[i] + a[i] * carry
            ol_ref[8 * i:8 * i + 8, sl] = hcur * gate[i]
            carry = hcur[7:8, :]
        hc_ref[0:1, sl] = carry

    for h in range(LRU_HEADS):
        project(h)
        recurrence(h, *conv_and_gates(h))

    ext_ref[0:hist, :] = ext_ref[ts:ts + hist, :]
    ol = ol_ref[...]
    nl_ref[g % 2] = (ol * _rms_scale(ol) * gl_ref[...]).astype(BF16)


def _mix(proj, o_attn, x2, mod, w_conv, b_conv, w_gates, b_a, b_x, lam, g_a, g_l, w_out, w_mlp_out,
         seq, ts=256):
    t, d = x2.shape
    c = LRU_HEADS * LRU_BLOCK
    ns = seq // ts
    nblk = t // ts
    cur = lambda g: jnp.minimum(g, nblk - 1)
    prev = lambda g: jnp.maximum(g - 1, 0)
    vec = pl.BlockSpec((1, c), lambda g: (0, 0))
    xr_col = 3 * ATTN_HEADS * HEAD_DIM // c
    f = w_mlp_out.shape[0]
    w2_chunk = pl.BlockSpec((f // nblk, d), lambda g: (cur(g), 0))
    return pl.pallas_call(
        functools.partial(_mix_kernel, ts=ts, ns=ns),
        grid=(nblk + 1,),
        in_specs=[
            pl.BlockSpec((ts, c), lambda g: (cur(g), xr_col)),
            pl.BlockSpec((ts, c), lambda g: (cur(g), xr_col + 1)),
            pl.BlockSpec((ts, c), lambda g: (prev(g), 0)),
            pl.BlockSpec((ts, d), lambda g: (prev(g), 0)),
            pl.BlockSpec((1, N_MOD, d), lambda g: (prev(g) // ns, 0, 0)),
            pl.BlockSpec((CONV_WIDTH, c), lambda g: (0, 0)),
            vec,
            pl.BlockSpec((LRU_HEADS, LRU_BLOCK, 2 * LRU_BLOCK), lambda g: (0, 0, 0)),
            vec, vec, vec, vec, vec,
            pl.BlockSpec((2 * c, d), lambda g: (0, 0)),
            w2_chunk,
        ],
        out_specs=[pl.BlockSpec((ts, d), lambda g: (prev(g), 0)), w2_chunk],
        out_shape=[jax.ShapeDtypeStruct((t, d), F32), jax.ShapeDtypeStruct(w_mlp_out.shape, BF16)],
        scratch_shapes=[pltpu.VMEM((ts + 8, c), F32), pltpu.VMEM((8, c), F32),
                        pltpu.VMEM((ts, c), F32), pltpu.VMEM((2, ts, c), BF16)],
        compiler_params=pltpu.CompilerParams(
            dimension_semantics=("arbitrary",), vmem_limit_bytes=VMEM_LIMIT),
        name="mix",
    )(proj, proj, o_attn, x2, mod, w_conv, b_conv, w_gates, b_a, b_x, lam, g_a, g_l, w_out, w_mlp_out)


def _mlp_kernel(x_ref, mod_ref, g_ref, w1_ref, w2_ref, gf_ref, o_ref, h_ref, *, row_blocks):
    j = pl.program_id(1)
    last = pl.num_programs(1) - 1
    rb = x_ref.shape[0] // row_blocks

    def expand(h):
        hid = jnp.dot(h, w1_ref[...], preferred_element_type=F32)
        hid = jnp.square(jnp.maximum(hid, 0.0)).astype(BF16)
        return jnp.dot(hid, w2_ref[...], preferred_element_type=F32)

    @pl.when(j == 0)
    def _():
        for r in range(row_blocks):
            rows = slice(r * rb, (r + 1) * rb)
            h = _adaln(x_ref[rows, :], g_ref[...], mod_ref[0, 3:4, :], mod_ref[0, 4:5, :])
            h_ref[rows, :] = h
            o_ref[rows, :] = expand(h)

    @pl.when(jnp.logical_and(j > 0, j < last))
    def _():
        o_ref[...] += expand(h_ref[...])

    @pl.when(j == last)
    def _():
        for r in range(row_blocks):
            rows = slice(r * rb, (r + 1) * rb)
            x2 = x_ref[rows, :] + mod_ref[0, 5:6, :] * (o_ref[rows, :] + expand(h_ref[rows, :]))
            o_ref[rows, :] = x2 * _rms_scale(x2) * gf_ref[...]


def _mlp(x1, mod, g, w1, w2, g_final, seq, tm=512, tf=1024):
    t, d = x1.shape
    f = w1.shape[1]
    assert f // tf >= 2, "first and last hidden-chunk steps are distinct code paths"
    return pl.pallas_call(
        functools.partial(_mlp_kernel, row_blocks=2),
        grid=(t // tm, f // tf),
        in_specs=[
            pl.BlockSpec((tm, d), lambda i, j: (i, 0)),
            pl.BlockSpec((1, N_MOD, d), lambda i, j: (i * tm // seq, 0, 0)),
            pl.BlockSpec((1, d), lambda i, j: (0, 0)),
            pl.BlockSpec((d, tf), lambda i, j: (0, j)),
            pl.BlockSpec((tf, d), lambda i, j: (j, 0)),
            pl.BlockSpec((1, d), lambda i, j: (0, 0)),
        ],
        out_specs=pl.BlockSpec((tm, d), lambda i, j: (i, 0)),
        out_shape=jax.ShapeDtypeStruct((t, d), F32),
        scratch_shapes=[pltpu.VMEM((tm, d), BF16)],
        compiler_params=pltpu.CompilerParams(
            dimension_semantics=("arbitrary", "arbitrary"), vmem_limit_bytes=VMEM_LIMIT),
        name="mlp",
    )(x1, mod, g, w1, w2, g_final)


def kernel(x, c, w_ada, b_ada, g_norm_mix, w_in, w_conv, b_conv, w_rg_a, b_rg_a, w_rg_x, b_rg_x,
           lru_lambda, g_attn_out, g_lru_out, w_out, g_norm_mlp, w_mlp_in, w_mlp_out, g_norm_final):
    batch, seq, d = x.shape
    assert w_ada.shape[0] == 1, "single layer: the final norm is fused into the MLP kernel"
    xt = x.reshape(batch * seq, d)
    c_pad = jnp.pad(c.astype(F32), ((0, 8 - batch), (0, 0)))
    mod, w_in_bf16 = _ada(c_pad, w_ada[0], b_ada, w_in[0])
    mod = mod[:batch].reshape(batch, N_MOD, d)
    proj = _inproj(xt, mod, g_norm_mix, w_in_bf16, seq)
    o_attn, w1, wo = _attention(proj, w_mlp_in[0], w_out[0], batch, seq)
    w_gates = jnp.concatenate([w_rg_a[0], w_rg_x[0]], axis=-1).astype(BF16)
    x1, w2 = _mix(proj, o_attn, xt, mod, w_conv[0], b_conv, w_gates, b_rg_a, b_rg_x, lru_lambda,
                  g_attn_out, g_lru_out, wo, w_mlp_out[0], seq)
    out = _mlp(x1, mod, g_norm_mlp, w1, w2, g_norm_final[None, :], seq)
    return out.reshape(batch, seq, d)
```

```python
import functools

import jax
import jax.numpy as jnp
from jax import lax
from jax.experimental import pallas as pl
from jax.experimental.pallas import tpu as pltpu

F32 = jnp.float32
BF16 = jnp.bfloat16

ATTN_HEADS = 8
HEAD_DIM = 128
LRU_HEADS = 8
LRU_BLOCK = 128
CONV_WIDTH = 4
LRU_C = 8.0
LRU_ROWS = 256
N_MOD = 6
EPS = 1e-6

LOG2_E = 1.4426950408889634
EXP2_UNDERFLOW = -151.0
MASKED_LOGIT = 1e30

VMEM_LIMIT = 56 * 1024 * 1024
SCORES_AHEAD = 2


def _softplus(x):
    return jnp.maximum(x, 0.0) + jnp.log1p(jnp.exp(-jnp.abs(x)))


def _rms_scale(x):
    return lax.rsqrt(jnp.mean(x * x, axis=-1, keepdims=True) + EPS)


def _modulation(c_ref, w_ref, b_ref):
    c = c_ref[...]
    c_act = c * jax.nn.sigmoid(c)
    return jnp.dot(c_act.astype(BF16), w_ref[...].astype(BF16),
                   preferred_element_type=F32) + b_ref[...]


def _ada_kernel(c_ref, w_ref, b_ref, winf_ref, o_ref, winb_ref):
    winb_ref[...] = winf_ref[...].astype(BF16)
    o_ref[...] = _modulation(c_ref, w_ref, b_ref)


def _ada(c_pad, w_ada, b_ada, w_in, n, tn=512):
    m, d = c_pad.shape
    steps = n // tn
    win_chunk = pl.BlockSpec((w_in.shape[0] // steps, w_in.shape[1]), lambda j: (j, 0))
    return pl.pallas_call(
        _ada_kernel,
        grid=(steps,),
        in_specs=[
            pl.BlockSpec((m, d), lambda j: (0, 0)),
            pl.BlockSpec((d, tn), lambda j: (0, j)),
            pl.BlockSpec((1, tn), lambda j: (0, j)),
            win_chunk,
        ],
        out_specs=[pl.BlockSpec((m, tn), lambda j: (0, j)), win_chunk],
        out_shape=[jax.ShapeDtypeStruct((m, n), F32), jax.ShapeDtypeStruct(w_in.shape, BF16)],
        compiler_params=pltpu.CompilerParams(
            dimension_semantics=("arbitrary",), vmem_limit_bytes=VMEM_LIMIT),
        name="ada",
    )(c_pad, w_ada, b_ada, w_in)


def _adaln(x, g, shift, scale):
    return ((x * _rms_scale(x) * g) * (1.0 + scale) + shift).astype(BF16)


def _inproj_kernel(x_ref, mod_ref, g_ref, w_ref, cs_ref, c_ref, wa_ref, ba_ref, o_ref, modr_ref,
                   h_ref, *, row_blocks):
    modr_ref[...] = _modulation(c_ref, wa_ref, ba_ref)
    j = pl.program_id(1)
    rb = x_ref.shape[0] // row_blocks

    def project(h):
        return (jnp.dot(h, w_ref[...], preferred_element_type=F32) * cs_ref[...]).astype(BF16)

    @pl.when(j == 0)
    def _():
        for r in range(row_blocks):
            rows = slice(r * rb, (r + 1) * rb)
            h = _adaln(x_ref[rows, :], g_ref[...], mod_ref[0, 0:1, :], mod_ref[0, 1:2, :])
            h_ref[rows, :] = h
            o_ref[rows, :] = project(h)

    @pl.when(j > 0)
    def _():
        o_ref[...] = project(h_ref[...])


def _inproj(x2, mod_mix, g, w_in, c_pad, w_ada, b_ada, n_done, seq, tm=1024, tn=1280):
    t, d = x2.shape
    n = w_in.shape[1]
    nt, nj = t // tm, n // tn
    col = lax.broadcasted_iota(jnp.int32, (1, n), 1)
    colscale = jnp.where(col < ATTN_HEADS * HEAD_DIM, HEAD_DIM ** -0.5 * LOG2_E, 1.0).astype(F32)
    m = c_pad.shape[0]
    n_rest = w_ada.shape[1] - n_done
    ta = n_rest // (nt * nj)
    ada_chunk = lambda rows: pl.BlockSpec((rows, ta), lambda i, j: (0, n_done // ta + i * nj + j))
    return pl.pallas_call(
        functools.partial(_inproj_kernel, row_blocks=4),
        grid=(nt, nj),
        in_specs=[
            pl.BlockSpec((tm, d), lambda i, j: (i, 0)),
            pl.BlockSpec((1, 2, d), lambda i, j: (i * tm // seq, 0, 0)),
            pl.BlockSpec((1, d), lambda i, j: (0, 0)),
            pl.BlockSpec((d, tn), lambda i, j: (0, j)),
            pl.BlockSpec((1, tn), lambda i, j: (0, j)),
            pl.BlockSpec((m, d), lambda i, j: (0, 0)),
            ada_chunk(d),
            ada_chunk(1),
        ],
        out_specs=[pl.BlockSpec((tm, tn), lambda i, j: (i, j)),
                   pl.BlockSpec((m, ta), lambda i, j: (0, i * nj + j))],
        out_shape=[jax.ShapeDtypeStruct((t, n), BF16), jax.ShapeDtypeStruct((m, n_rest), F32)],
        scratch_shapes=[pltpu.VMEM((tm, d), BF16)],
        compiler_params=pltpu.CompilerParams(
            dimension_semantics=("arbitrary", "arbitrary"), vmem_limit_bytes=VMEM_LIMIT),
        name="inproj",
    )(x2, mod_mix, g, w_in, colscale, c_pad, w_ada, b_ada)


def _attn_kernel(q_ref, k_ref, v_ref, u_ref, w1f_ref, wof_ref, o_ref, w1b_ref, wob_ref, c_ref,
                 *, tq, nq):
    w1b_ref[...] = w1f_ref[...].astype(BF16)
    wob_ref[...] = wof_ref[...].astype(BF16)

    u = u_ref[...]
    rows = lax.broadcasted_iota(jnp.int32, (tq, tq), 0)
    cols = lax.broadcasted_iota(jnp.int32, (tq, tq), 1)
    causal = cols < rows

    def scores(qb, start, mask):
        k = k_ref[pl.ds(start, tq), :]
        z = lax.dot_general(qb, k, (((1,), (1,)), ((), ())), preferred_element_type=F32)
        if mask is not None:
            z = jnp.where(mask, z, -MASKED_LOGIT)
        t = jnp.log2(1.0 + jnp.exp2(-jnp.abs(z)))
        log_beta = jnp.minimum(z, 0.0) - t
        return log_beta, log_beta - z

    def weights(log_beta, log_stay, c):
        after = jnp.dot(log_stay.astype(BF16), u, preferred_element_type=F32)
        e = after + log_beta if c is None else after + (log_beta + c)
        return jnp.exp2(e).astype(BF16), after[:, 0:1] + log_stay[:, 0:1]

    pairs = [(i, j) for i in range(nq) for j in ((i, i - 1) if i > 0 else (i,))]
    carry = {}
    acc = {}

    def finish(i, j, lb, ls):
        v = v_ref[j * tq:(j + 1) * tq, :]
        if j == i:
            w, carry[i] = weights(lb, ls, None)
            acc[i] = jnp.dot(w, v, preferred_element_type=F32)
        else:
            w, rs = weights(lb, ls, carry[i])
            acc[i] = acc[i] + jnp.dot(w, v, preferred_element_type=F32)
            carry[i] = carry[i] + rs
        if j == max(i - 1, 0):
            o_ref[i * tq:(i + 1) * tq, :] = acc.pop(i)
            c = carry.pop(i)
            if i > 1:
                c_ref[i] = c

    pending = []
    for i, j in pairs:
        pending.append((i, j) + scores(q_ref[i * tq:(i + 1) * tq, :], j * tq,
                                       causal if j == i else None))
        if len(pending) > SCORES_AHEAD:
            finish(*pending.pop(0))
    for p in pending:
        finish(*p)

    if nq > 2:
        @pl.when(jnp.max(c_ref[2:nq]) > EXP2_UNDERFLOW)
        def _():
            for i in range(2, nq):
                lo_row = i * tq
                qb = q_ref[lo_row:lo_row + tq, :]

                def cond(state):
                    j, alive, _ = state
                    return jnp.logical_and(j >= 0, alive)

                def body(state):
                    j, _, c = state
                    start = pl.multiple_of(j * tq, tq)
                    lb, ls = scores(qb, start, None)
                    w, rs = weights(lb, ls, c)
                    o_ref[lo_row:lo_row + tq, :] += jnp.dot(
                        w, v_ref[pl.ds(start, tq), :], preferred_element_type=F32)
                    c = c + rs
                    return j - 1, jnp.max(c) > EXP2_UNDERFLOW, c

                c0 = c_ref[i]
                lax.while_loop(cond, body, (i - 2, jnp.max(c0) > EXP2_UNDERFLOW, c0))


def _attention(proj, w_mlp_in, w_out, batch, seq, tq=256):
    t = proj.shape[0]
    nq = seq // tq
    r = lax.broadcasted_iota(jnp.int32, (tq, tq), 0)
    s = lax.broadcasted_iota(jnp.int32, (tq, tq), 1)
    u = (r > s).astype(BF16)
    h = ATTN_HEADS
    steps = batch * h
    d, f = w_mlp_in.shape
    w1_chunk = pl.BlockSpec((d, f // steps), lambda b, hh: (0, b * h + hh))
    wo_chunk = pl.BlockSpec((w_out.shape[0] // steps, w_out.shape[1]), lambda b, hh: (b * h + hh, 0))
    head = lambda off: pl.BlockSpec((seq, HEAD_DIM), lambda b, hh: (b, off + hh))
    return pl.pallas_call(
        functools.partial(_attn_kernel, tq=tq, nq=nq),
        grid=(batch, h),
        in_specs=[head(0), head(h), head(2 * h), pl.BlockSpec((tq, tq), lambda b, hh: (0, 0)),
                  w1_chunk, wo_chunk],
        out_specs=[head(0), w1_chunk, wo_chunk],
        out_shape=[jax.ShapeDtypeStruct((t, h * HEAD_DIM), F32),
                   jax.ShapeDtypeStruct(w_mlp_in.shape, BF16),
                   jax.ShapeDtypeStruct(w_out.shape, BF16)],
        scratch_shapes=[pltpu.VMEM((nq, tq, 1), F32)],
        compiler_params=pltpu.CompilerParams(
            dimension_semantics=("arbitrary", "arbitrary"), vmem_limit_bytes=VMEM_LIMIT),
        name="attn",
    )(proj, proj, proj, u, w_mlp_in, w_out)


def _mix_kernel(xr_ref, xg_ref, oa_ref, x_ref, mod_ref, wc_ref, bc_ref, wg_ref, ba_ref, bx_ref,
                lam_ref, ga_ref, gl_ref, wo_ref, w2f_ref, o_ref, w2b_ref,
                ext_ref, hc_ref, ol_ref, nl_ref, *, ts, ns):
    g = pl.program_id(0)
    hist = 8
    half = oa_ref.shape[1]

    w2b_ref[...] = w2f_ref[...].astype(BF16)

    @pl.when(g % ns == 0)
    def _():
        ext_ref[0:hist, :] = jnp.zeros((hist, ext_ref.shape[1]), F32)
        hc_ref[...] = jnp.zeros_like(hc_ref)

    @pl.when(g == 0)
    def _():
        nl_ref[...] = jnp.zeros_like(nl_ref)

    oa = oa_ref[...]
    na = (oa * _rms_scale(oa) * ga_ref[...]).astype(BF16)
    mixed = jnp.concatenate([na, nl_ref[(g + 1) % 2]], axis=1)
    ncol = o_ref.shape[1] // LRU_HEADS

    def project(h):
        cs = slice(h * ncol, (h + 1) * ncol)
        y = jnp.dot(mixed, wo_ref[:, cs], preferred_element_type=F32)
        o_ref[:, cs] = x_ref[:, cs] + mod_ref[0, 2:3, cs] * y

    ext_ref[hist:hist + ts, :] = xr_ref[...].astype(F32)
    nv = LRU_ROWS // 8
    sub = lax.broadcasted_iota(jnp.int32, (nv, 8, LRU_BLOCK), 1)

    def conv_and_gates(h, r0):
        sl = slice(h * LRU_BLOCK, (h + 1) * LRU_BLOCK)
        wc = wc_ref[:, sl]
        xe = ext_ref[r0:r0 + hist + LRU_ROWS, sl].reshape(nv + 1, 8, LRU_BLOCK)
        xc = bc_ref[:, sl] + wc[CONV_WIDTH - 1:CONV_WIDTH, :] * xe[1:]
        for d in range(1, CONV_WIDTH):
            rolled = pltpu.roll(xe, d, 1)
            tap = CONV_WIDTH - 1 - d
            xc = xc + wc[tap:tap + 1, :] * jnp.where(sub >= d, rolled[1:], rolled[:-1])
        xc = xc.reshape(LRU_ROWS, LRU_BLOCK)
        return xc, jnp.dot(xc.astype(BF16), wg_ref[h], preferred_element_type=F32)

    def recurrence(h, r0, carry, xc, gates):
        sl = slice(h * LRU_BLOCK, (h + 1) * LRU_BLOCK)
        r = jax.nn.sigmoid(gates[:, :LRU_BLOCK] + ba_ref[:, sl])
        gi = jax.nn.sigmoid(gates[:, LRU_BLOCK:] + bx_ref[:, sl])
        log_a = (-LRU_C) * r * _softplus(-lam_ref[:, sl])
        a = jnp.exp(log_a)
        one_minus_a2 = -jnp.tanh(log_a) * (1.0 + a * a)
        b = jnp.exp2(jnp.log(one_minus_a2) * (0.5 * LOG2_E)) * (gi * xc)
        a = a.reshape(nv, 8, LRU_BLOCK)
        b = b.reshape(nv, 8, LRU_BLOCK)
        for d in (1, 2, 4):
            keep = sub >= d
            a_prev = pltpu.roll(a, d, 1)
            b_prev = pltpu.roll(b, d, 1)
            b = b + a * jnp.where(keep, b_prev, 0.0)
            a = a * jnp.where(keep, a_prev, 1.0)
        gate = jax.nn.gelu(xg_ref[r0:r0 + LRU_ROWS, sl].astype(F32), approximate=True)
        gate = gate.reshape(nv, 8, LRU_BLOCK)
        for i in range(nv):
            hcur = b[i] + a[i] * carry
            ol_ref[r0 + 8 * i:r0 + 8 * i + 8, sl] = hcur * gate[i]
            carry = hcur[7:8, :]
        return carry

    for h in range(LRU_HEADS):
        project(h)
        sl = slice(h * LRU_BLOCK, (h + 1) * LRU_BLOCK)
        carry = hc_ref[0:1, sl]
        for r0 in range(0, ts, LRU_ROWS):
            carry = recurrence(h, r0, carry, *conv_and_gates(h, r0))
        hc_ref[0:1, sl] = carry

    ext_ref[0:hist, :] = ext_ref[ts:ts + hist, :]
    ol = ol_ref[...]
    nl_ref[g % 2] = (ol * _rms_scale(ol) * gl_ref[...]).astype(BF16)


def _mix(proj, o_attn, x2, mod, w_conv, b_conv, w_gates, b_a, b_x, lam, g_a, g_l, w_out, w_mlp_out,
         seq, ts=256):
    t, d = x2.shape
    c = LRU_HEADS * LRU_BLOCK
    ns = seq // ts
    nblk = t // ts
    cur = lambda g: jnp.minimum(g, nblk - 1)
    prev = lambda g: jnp.maximum(g - 1, 0)
    vec = pl.BlockSpec((1, c), lambda g: (0, 0))
    xr_col = 3 * ATTN_HEADS * HEAD_DIM // c
    f = w_mlp_out.shape[0]
    w2_chunk = pl.BlockSpec((f // nblk, d), lambda g: (cur(g), 0))
    return pl.pallas_call(
        functools.partial(_mix_kernel, ts=ts, ns=ns),
        grid=(nblk + 1,),
        in_specs=[
            pl.BlockSpec((ts, c), lambda g: (cur(g), xr_col)),
            pl.BlockSpec((ts, c), lambda g: (cur(g), xr_col + 1)),
            pl.BlockSpec((ts, c), lambda g: (prev(g), 0)),
            pl.BlockSpec((ts, d), lambda g: (prev(g), 0)),
            pl.BlockSpec((1, N_MOD, d), lambda g: (prev(g) // ns, 0, 0)),
            pl.BlockSpec((CONV_WIDTH, c), lambda g: (0, 0)),
            vec,
            pl.BlockSpec((LRU_HEADS, LRU_BLOCK, 2 * LRU_BLOCK), lambda g: (0, 0, 0)),
            vec, vec, vec, vec, vec,
            pl.BlockSpec((2 * c, d), lambda g: (0, 0)),
            w2_chunk,
        ],
        out_specs=[pl.BlockSpec((ts, d), lambda g: (prev(g), 0)), w2_chunk],
        out_shape=[jax.ShapeDtypeStruct((t, d), F32), jax.ShapeDtypeStruct(w_mlp_out.shape, BF16)],
        scratch_shapes=[pltpu.VMEM((ts + 8, c), F32), pltpu.VMEM((8, c), F32),
                        pltpu.VMEM((ts, c), F32), pltpu.VMEM((2, ts, c), BF16)],
        compiler_params=pltpu.CompilerParams(
            dimension_semantics=("arbitrary",), vmem_limit_bytes=VMEM_LIMIT),
        name="mix",
    )(proj, proj, o_attn, x2, mod, w_conv, b_conv, w_gates, b_a, b_x, lam, g_a, g_l, w_out, w_mlp_out)


def _mlp_kernel(x_ref, mod_ref, g_ref, w1_ref, w2_ref, gf_ref, o_ref, h_ref, *, row_blocks):
    j = pl.program_id(1)
    last = pl.num_programs(1) - 1
    rb = x_ref.shape[0] // row_blocks

    def expand(h):
        hid = jnp.dot(h, w1_ref[...], preferred_element_type=F32)
        hid = jnp.square(jnp.maximum(hid, 0.0)).astype(BF16)
        return jnp.dot(hid, w2_ref[...], preferred_element_type=F32)

    @pl.when(j == 0)
    def _():
        for r in range(row_blocks):
            rows = slice(r * rb, (r + 1) * rb)
            h = _adaln(x_ref[rows, :], g_ref[...], mod_ref[0, 3:4, :], mod_ref[0, 4:5, :])
            h_ref[rows, :] = h
            o_ref[rows, :] = expand(h)

    @pl.when(jnp.logical_and(j > 0, j < last))
    def _():
        o_ref[...] += expand(h_ref[...])

    @pl.when(j == last)
    def _():
        for r in range(row_blocks):
            rows = slice(r * rb, (r + 1) * rb)
            x2 = x_ref[rows, :] + mod_ref[0, 5:6, :] * (o_ref[rows, :] + expand(h_ref[rows, :]))
            o_ref[rows, :] = x2 * _rms_scale(x2) * gf_ref[...]


def _mlp(x1, mod, g, w1, w2, g_final, seq, tm=512, tf=1024):
    t, d = x1.shape
    f = w1.shape[1]
    assert f // tf >= 2, "first and last hidden-chunk steps are distinct code paths"
    return pl.pallas_call(
        functools.partial(_mlp_kernel, row_blocks=2),
        grid=(t // tm, f // tf),
        in_specs=[
            pl.BlockSpec((tm, d), lambda i, j: (i, 0)),
            pl.BlockSpec((1, N_MOD, d), lambda i, j: (i * tm // seq, 0, 0)),
            pl.BlockSpec((1, d), lambda i, j: (0, 0)),
            pl.BlockSpec((d, tf), lambda i, j: (0, j)),
            pl.BlockSpec((tf, d), lambda i, j: (j, 0)),
            pl.BlockSpec((1, d), lambda i, j: (0, 0)),
        ],
        out_specs=pl.BlockSpec((tm, d), lambda i, j: (i, 0)),
        out_shape=jax.ShapeDtypeStruct((t, d), F32),
        scratch_shapes=[pltpu.VMEM((tm, d), BF16)],
        compiler_params=pltpu.CompilerParams(
            dimension_semantics=("arbitrary", "arbitrary"), vmem_limit_bytes=VMEM_LIMIT),
        name="mlp",
    )(x1, mod, g, w1, w2, g_final)


def kernel(x, c, w_ada, b_ada, g_norm_mix, w_in, w_conv, b_conv, w_rg_a, b_rg_a, w_rg_x, b_rg_x,
           lru_lambda, g_attn_out, g_lru_out, w_out, g_norm_mlp, w_mlp_in, w_mlp_out, g_norm_final):
    batch, seq, d = x.shape
    assert w_ada.shape[0] == 1, "single layer: the final norm is fused into the MLP kernel"
    xt = x.reshape(batch * seq, d)
    c_pad = jnp.pad(c.astype(F32), ((0, 8 - batch), (0, 0)))
    mod_head, w_in_bf16 = _ada(c_pad, w_ada[0], b_ada, w_in[0], 2 * d)
    proj, mod_rest = _inproj(xt, mod_head[:batch].reshape(batch, 2, d), g_norm_mix, w_in_bf16,
                             c_pad, w_ada[0], b_ada, 2 * d, seq)
    mod = jnp.concatenate([mod_head, mod_rest], axis=1)[:batch].reshape(batch, N_MOD, d)
    o_attn, w1, wo = _attention(proj, w_mlp_in[0], w_out[0], batch, seq)
    w_gates = jnp.concatenate([w_rg_a[0], w_rg_x[0]], axis=-1).astype(BF16)
    x1, w2 = _mix(proj, o_attn, xt, mod, w_conv[0], b_conv, w_gates, b_rg_a, b_rg_x, lru_lambda,
                  g_attn_out, g_lru_out, wo, w_mlp_out[0], seq)
    out = _mlp(x1, mod, g_norm_mlp, w1, w2, g_norm_final[None, :], seq)
    return out.reshape(batch, seq, d)
```

```python
import functools

import jax
import jax.numpy as jnp
from jax import lax
from jax.experimental import pallas as pl
from jax.experimental.pallas import tpu as pltpu

F32 = jnp.float32
BF16 = jnp.bfloat16

ATTN_HEADS = 8
HEAD_DIM = 128
LRU_HEADS = 8
LRU_BLOCK = 128
CONV_WIDTH = 4
LRU_C = 8.0
LRU_ROWS = 256
N_MOD = 6
EPS = 1e-6

LOG2_E = 1.4426950408889634
EXP2_UNDERFLOW = -151.0
MASKED_LOGIT = 1e30

VMEM_LIMIT = 56 * 1024 * 1024
SCORES_AHEAD = 2


def _softplus(x):
    return jnp.maximum(x, 0.0) + jnp.log1p(jnp.exp(-jnp.abs(x)))


def _rms_scale(x):
    return lax.rsqrt(jnp.mean(x * x, axis=-1, keepdims=True) + EPS)


def _modulation(c_ref, w_ref, b_ref):
    c = c_ref[...]
    c_act = c * jax.nn.sigmoid(c)
    return jnp.dot(c_act.astype(BF16), w_ref[...].astype(BF16),
                   preferred_element_type=F32) + b_ref[...]


def _ada_kernel(c_ref, w_ref, b_ref, winf_ref, o_ref, winb_ref):
    winb_ref[...] = winf_ref[...].astype(BF16)
    o_ref[...] = _modulation(c_ref, w_ref, b_ref)


def _ada(c_pad, w_ada, b_ada, w_in, n, tn=512):
    m, d = c_pad.shape
    steps = n // tn
    win_chunk = pl.BlockSpec((w_in.shape[0] // steps, w_in.shape[1]), lambda j: (j, 0))
    return pl.pallas_call(
        _ada_kernel,
        grid=(steps,),
        in_specs=[
            pl.BlockSpec((m, d), lambda j: (0, 0)),
            pl.BlockSpec((d, tn), lambda j: (0, j)),
            pl.BlockSpec((1, tn), lambda j: (0, j)),
            win_chunk,
        ],
        out_specs=[pl.BlockSpec((m, tn), lambda j: (0, j)), win_chunk],
        out_shape=[jax.ShapeDtypeStruct((m, n), F32), jax.ShapeDtypeStruct(w_in.shape, BF16)],
        compiler_params=pltpu.CompilerParams(
            dimension_semantics=("arbitrary",), vmem_limit_bytes=VMEM_LIMIT),
        name="ada",
    )(c_pad, w_ada, b_ada, w_in)


def _adaln(x, g, shift, scale):
    return ((x * _rms_scale(x) * g) * (1.0 + scale) + shift).astype(BF16)


def _inproj_kernel(x_ref, mod_ref, g_ref, w_ref, cs_ref, o_ref, h_ref, *, row_blocks):
    j = pl.program_id(1)
    rb = x_ref.shape[0] // row_blocks

    def project(h):
        return (jnp.dot(h, w_ref[...], preferred_element_type=F32) * cs_ref[...]).astype(BF16)

    @pl.when(j == 0)
    def _():
        for r in range(row_blocks):
            rows = slice(r * rb, (r + 1) * rb)
            h = _adaln(x_ref[rows, :], g_ref[...], mod_ref[0, 0:1, :], mod_ref[0, 1:2, :])
            h_ref[rows, :] = h
            o_ref[rows, :] = project(h)

    @pl.when(j > 0)
    def _():
        o_ref[...] = project(h_ref[...])


def _inproj(x2, mod_mix, g, w_in, seq, tm=1024, tn=1280):
    t, d = x2.shape
    n = w_in.shape[1]
    nt, nj = t // tm, n // tn
    col = lax.broadcasted_iota(jnp.int32, (1, n), 1)
    colscale = jnp.where(col < ATTN_HEADS * HEAD_DIM, HEAD_DIM ** -0.5 * LOG2_E, 1.0).astype(F32)
    return pl.pallas_call(
        functools.partial(_inproj_kernel, row_blocks=4),
        grid=(nt, nj),
        in_specs=[
            pl.BlockSpec((tm, d), lambda i, j: (i, 0)),
            pl.BlockSpec((1, 2, d), lambda i, j: (i * tm // seq, 0, 0)),
            pl.BlockSpec((1, d), lambda i, j: (0, 0)),
            pl.BlockSpec((d, tn), lambda i, j: (0, j)),
            pl.BlockSpec((1, tn), lambda i, j: (0, j)),
        ],
        out_specs=pl.BlockSpec((tm, tn), lambda i, j: (i, j)),
        out_shape=jax.ShapeDtypeStruct((t, n), BF16),
        scratch_shapes=[pltpu.VMEM((tm, d), BF16)],
        compiler_params=pltpu.CompilerParams(
            dimension_semantics=("arbitrary", "arbitrary"), vmem_limit_bytes=VMEM_LIMIT),
        name="inproj",
    )(x2, mod_mix, g, w_in, colscale)


def _attn_kernel(q_ref, k_ref, v_ref, u_ref, w1f_ref, wof_ref, cond_ref, wa_ref, ba_ref,
                 o_ref, w1b_ref, wob_ref, modr_ref, c_ref, *, tq, nq):
    w1b_ref[...] = w1f_ref[...].astype(BF16)
    wob_ref[...] = wof_ref[...].astype(BF16)
    modr_ref[...] = _modulation(cond_ref, wa_ref, ba_ref)

    u = u_ref[...]
    rows = lax.broadcasted_iota(jnp.int32, (tq, tq), 0)
    cols = lax.broadcasted_iota(jnp.int32, (tq, tq), 1)
    causal = cols < rows

    def scores(qb, start, mask):
        k = k_ref[pl.ds(start, tq), :]
        z = lax.dot_general(qb, k, (((1,), (1,)), ((), ())), preferred_element_type=F32)
        if mask is not None:
            z = jnp.where(mask, z, -MASKED_LOGIT)
        t = jnp.log2(1.0 + jnp.exp2(-jnp.abs(z)))
        log_beta = jnp.minimum(z, 0.0) - t
        return log_beta, log_beta - z

    def weights(log_beta, log_stay, c):
        after = jnp.dot(log_stay.astype(BF16), u, preferred_element_type=F32)
        e = after + log_beta if c is None else after + (log_beta + c)
        return jnp.exp2(e).astype(BF16), after[:, 0:1] + log_stay[:, 0:1]

    pairs = [(i, j) for i in range(nq) for j in ((i, i - 1) if i > 0 else (i,))]
    carry = {}
    acc = {}

    def finish(i, j, lb, ls):
        v = v_ref[j * tq:(j + 1) * tq, :]
        if j == i:
            w, carry[i] = weights(lb, ls, None)
            acc[i] = jnp.dot(w, v, preferred_element_type=F32)
        else:
            w, rs = weights(lb, ls, carry[i])
            acc[i] = acc[i] + jnp.dot(w, v, preferred_element_type=F32)
            carry[i] = carry[i] + rs
        if j == max(i - 1, 0):
            o_ref[i * tq:(i + 1) * tq, :] = acc.pop(i)
            c = carry.pop(i)
            if i > 1:
                c_ref[i] = c

    pending = []
    for i, j in pairs:
        pending.append((i, j) + scores(q_ref[i * tq:(i + 1) * tq, :], j * tq,
                                       causal if j == i else None))
        if len(pending) > SCORES_AHEAD:
            finish(*pending.pop(0))
    for p in pending:
        finish(*p)

    if nq > 2:
        @pl.when(jnp.max(c_ref[2:nq]) > EXP2_UNDERFLOW)
        def _():
            for i in range(2, nq):
                lo_row = i * tq
                qb = q_ref[lo_row:lo_row + tq, :]

                def cond(state):
                    j, alive, _ = state
                    return jnp.logical_and(j >= 0, alive)

                def body(state):
                    j, _, c = state
                    start = pl.multiple_of(j * tq, tq)
                    lb, ls = scores(qb, start, None)
                    w, rs = weights(lb, ls, c)
                    o_ref[lo_row:lo_row + tq, :] += jnp.dot(
                        w, v_ref[pl.ds(start, tq), :], preferred_element_type=F32)
                    c = c + rs
                    return j - 1, jnp.max(c) > EXP2_UNDERFLOW, c

                c0 = c_ref[i]
                lax.while_loop(cond, body, (i - 2, jnp.max(c0) > EXP2_UNDERFLOW, c0))


def _attention(proj, w_mlp_in, w_out, c_pad, w_ada, b_ada, n_done, batch, seq, tq=256):
    t = proj.shape[0]
    nq = seq // tq
    r = lax.broadcasted_iota(jnp.int32, (tq, tq), 0)
    s = lax.broadcasted_iota(jnp.int32, (tq, tq), 1)
    u = (r > s).astype(BF16)
    h = ATTN_HEADS
    steps = batch * h
    d, f = w_mlp_in.shape
    w1_chunk = pl.BlockSpec((d, f // steps), lambda b, hh: (0, b * h + hh))
    wo_chunk = pl.BlockSpec((w_out.shape[0] // steps, w_out.shape[1]), lambda b, hh: (b * h + hh, 0))
    head = lambda off: pl.BlockSpec((seq, HEAD_DIM), lambda b, hh: (b, off + hh))
    m = c_pad.shape[0]
    n_rest = w_ada.shape[1] - n_done
    ta = n_rest // steps
    ada_chunk = lambda rows: pl.BlockSpec((rows, ta), lambda b, hh: (0, n_done // ta + b * h + hh))
    return pl.pallas_call(
        functools.partial(_attn_kernel, tq=tq, nq=nq),
        grid=(batch, h),
        in_specs=[head(0), head(h), head(2 * h), pl.BlockSpec((tq, tq), lambda b, hh: (0, 0)),
                  w1_chunk, wo_chunk, pl.BlockSpec((m, d), lambda b, hh: (0, 0)),
                  ada_chunk(d), ada_chunk(1)],
        out_specs=[head(0), w1_chunk, wo_chunk,
                   pl.BlockSpec((m, ta), lambda b, hh: (0, b * h + hh))],
        out_shape=[jax.ShapeDtypeStruct((t, h * HEAD_DIM), F32),
                   jax.ShapeDtypeStruct(w_mlp_in.shape, BF16),
                   jax.ShapeDtypeStruct(w_out.shape, BF16),
                   jax.ShapeDtypeStruct((m, n_rest), F32)],
        scratch_shapes=[pltpu.VMEM((nq, tq, 1), F32)],
        compiler_params=pltpu.CompilerParams(
            dimension_semantics=("arbitrary", "arbitrary"), vmem_limit_bytes=VMEM_LIMIT),
        name="attn",
    )(proj, proj, proj, u, w_mlp_in, w_out, c_pad, w_ada, b_ada)


def _mix_kernel(xr_ref, xg_ref, oa_ref, x_ref, mod_ref, wc_ref, bc_ref, wg_ref, ba_ref, bx_ref,
                lam_ref, ga_ref, gl_ref, wo_ref, w2f_ref, o_ref, w2b_ref,
                ext_ref, hc_ref, ol_ref, nl_ref, *, ts, ns):
    g = pl.program_id(0)
    hist = 8
    half = oa_ref.shape[1]

    w2b_ref[...] = w2f_ref[...].astype(BF16)

    @pl.when(g % ns == 0)
    def _():
        ext_ref[0:hist, :] = jnp.zeros((hist, ext_ref.shape[1]), F32)
        hc_ref[...] = jnp.zeros_like(hc_ref)

    @pl.when(g == 0)
    def _():
        nl_ref[...] = jnp.zeros_like(nl_ref)

    oa = oa_ref[...]
    na = (oa * _rms_scale(oa) * ga_ref[...]).astype(BF16)
    mixed = jnp.concatenate([na, nl_ref[(g + 1) % 2]], axis=1)
    ncol = o_ref.shape[1] // LRU_HEADS

    def project(h):
        cs = slice(h * ncol, (h + 1) * ncol)
        y = jnp.dot(mixed, wo_ref[:, cs], preferred_element_type=F32)
        o_ref[:, cs] = x_ref[:, cs] + mod_ref[0, 2:3, cs] * y

    ext_ref[hist:hist + ts, :] = xr_ref[...].astype(F32)
    nv = LRU_ROWS // 8
    sub = lax.broadcasted_iota(jnp.int32, (nv, 8, LRU_BLOCK), 1)

    def conv_and_gates(h, r0):
        sl = slice(h * LRU_BLOCK, (h + 1) * LRU_BLOCK)
        wc = wc_ref[:, sl]
        xe = ext_ref[r0:r0 + hist + LRU_ROWS, sl].reshape(nv + 1, 8, LRU_BLOCK)
        xc = bc_ref[:, sl] + wc[CONV_WIDTH - 1:CONV_WIDTH, :] * xe[1:]
        for d in range(1, CONV_WIDTH):
            rolled = pltpu.roll(xe, d, 1)
            tap = CONV_WIDTH - 1 - d
            xc = xc + wc[tap:tap + 1, :] * jnp.where(sub >= d, rolled[1:], rolled[:-1])
        xc = xc.reshape(LRU_ROWS, LRU_BLOCK)
        return xc, jnp.dot(xc.astype(BF16), wg_ref[h], preferred_element_type=F32)

    def recurrence(h, r0, carry, xc, gates):
        sl = slice(h * LRU_BLOCK, (h + 1) * LRU_BLOCK)
        r = jax.nn.sigmoid(gates[:, :LRU_BLOCK] + ba_ref[:, sl])
        gi = jax.nn.sigmoid(gates[:, LRU_BLOCK:] + bx_ref[:, sl])
        log_a = (-LRU_C) * r * _softplus(-lam_ref[:, sl])
        a = jnp.exp(log_a)
        one_minus_a2 = -jnp.tanh(log_a) * (1.0 + a * a)
        b = jnp.exp2(jnp.log(one_minus_a2) * (0.5 * LOG2_E)) * (gi * xc)
        a = a.reshape(nv, 8, LRU_BLOCK)
        b = b.reshape(nv, 8, LRU_BLOCK)
        for d in (1, 2, 4):
            keep = sub >= d
            a_prev = pltpu.roll(a, d, 1)
            b_prev = pltpu.roll(b, d, 1)
            b = b + a * jnp.where(keep, b_prev, 0.0)
            a = a * jnp.where(keep, a_prev, 1.0)
        gate = jax.nn.gelu(xg_ref[r0:r0 + LRU_ROWS, sl].astype(F32), approximate=True)
        gate = gate.reshape(nv, 8, LRU_BLOCK)
        for i in range(nv):
            hcur = b[i] + a[i] * carry
            ol_ref[r0 + 8 * i:r0 + 8 * i + 8, sl] = hcur * gate[i]
            carry = hcur[7:8, :]
        return carry

    for h in range(LRU_HEADS):
        project(h)
        sl = slice(h * LRU_BLOCK, (h + 1) * LRU_BLOCK)
        carry = hc_ref[0:1, sl]
        for r0 in range(0, ts, LRU_ROWS):
            carry = recurrence(h, r0, carry, *conv_and_gates(h, r0))
        hc_ref[0:1, sl] = carry

    ext_ref[0:hist, :] = ext_ref[ts:ts + hist, :]
    ol = ol_ref[...]
    nl_ref[g % 2] = (ol * _rms_scale(ol) * gl_ref[...]).astype(BF16)


def _mix(proj, o_attn, x2, mod, w_conv, b_conv, w_gates, b_a, b_x, lam, g_a, g_l, w_out, w_mlp_out,
         seq, ts=256):
    t, d = x2.shape
    c = LRU_HEADS * LRU_BLOCK
    ns = seq // ts
    nblk = t // ts
    cur = lambda g: jnp.minimum(g, nblk - 1)
    prev = lambda g: jnp.maximum(g - 1, 0)
    vec = pl.BlockSpec((1, c), lambda g: (0, 0))
    xr_col = 3 * ATTN_HEADS * HEAD_DIM // c
    f = w_mlp_out.shape[0]
    w2_chunk = pl.BlockSpec((f // nblk, d), lambda g: (cur(g), 0))
    return pl.pallas_call(
        functools.partial(_mix_kernel, ts=ts, ns=ns),
        grid=(nblk + 1,),
        in_specs=[
            pl.BlockSpec((ts, c), lambda g: (cur(g), xr_col)),
            pl.BlockSpec((ts, c), lambda g: (cur(g), xr_col + 1)),
            pl.BlockSpec((ts, c), lambda g: (prev(g), 0)),
            pl.BlockSpec((ts, d), lambda g: (prev(g), 0)),
            pl.BlockSpec((1, N_MOD, d), lambda g: (prev(g) // ns, 0, 0)),
            pl.BlockSpec((CONV_WIDTH, c), lambda g: (0, 0)),
            vec,
            pl.BlockSpec((LRU_HEADS, LRU_BLOCK, 2 * LRU_BLOCK), lambda g: (0, 0, 0)),
            vec, vec, vec, vec, vec,
            pl.BlockSpec((2 * c, d), lambda g: (0, 0)),
            w2_chunk,
        ],
        out_specs=[pl.BlockSpec((ts, d), lambda g: (prev(g), 0)), w2_chunk],
        out_shape=[jax.ShapeDtypeStruct((t, d), F32), jax.ShapeDtypeStruct(w_mlp_out.shape, BF16)],
        scratch_shapes=[pltpu.VMEM((ts + 8, c), F32), pltpu.VMEM((8, c), F32),
                        pltpu.VMEM((ts, c), F32), pltpu.VMEM((2, ts, c), BF16)],
        compiler_params=pltpu.CompilerParams(
            dimension_semantics=("arbitrary",), vmem_limit_bytes=VMEM_LIMIT),
        name="mix",
    )(proj, proj, o_attn, x2, mod, w_conv, b_conv, w_gates, b_a, b_x, lam, g_a, g_l, w_out, w_mlp_out)


def _mlp_kernel(x_ref, mod_ref, g_ref, w1_ref, w2_ref, gf_ref, o_ref, h_ref, *, row_blocks):
    j = pl.program_id(1)
    last = pl.num_programs(1) - 1
    rb = x_ref.shape[0] // row_blocks

    def expand(h):
        hid = jnp.dot(h, w1_ref[...], preferred_element_type=F32)
        hid = jnp.square(jnp.maximum(hid, 0.0)).astype(BF16)
        return jnp.dot(hid, w2_ref[...], preferred_element_type=F32)

    @pl.when(j == 0)
    def _():
        for r in range(row_blocks):
            rows = slice(r * rb, (r + 1) * rb)
            h = _adaln(x_ref[rows, :], g_ref[...], mod_ref[0, 3:4, :], mod_ref[0, 4:5, :])
            h_ref[rows, :] = h
            o_ref[rows, :] = expand(h)

    @pl.when(jnp.logical_and(j > 0, j < last))
    def _():
        o_ref[...] += expand(h_ref[...])

    @pl.when(j == last)
    def _():
        for r in range(row_blocks):
            rows = slice(r * rb, (r + 1) * rb)
            x2 = x_ref[rows, :] + mod_ref[0, 5:6, :] * (o_ref[rows, :] + expand(h_ref[rows, :]))
            o_ref[rows, :] = x2 * _rms_scale(x2) * gf_ref[...]


def _mlp(x1, mod, g, w1, w2, g_final, seq, tm=512, tf=1024):
    t, d = x1.shape
    f = w1.shape[1]
    assert f // tf >= 2, "first and last hidden-chunk steps are distinct code paths"
    return pl.pallas_call(
        functools.partial(_mlp_kernel, row_blocks=2),
        grid=(t // tm, f // tf),
        in_specs=[
            pl.BlockSpec((tm, d), lambda i, j: (i, 0)),
            pl.BlockSpec((1, N_MOD, d), lambda i, j: (i * tm // seq, 0, 0)),
            pl.BlockSpec((1, d), lambda i, j: (0, 0)),
            pl.BlockSpec((d, tf), lambda i, j: (0, j)),
            pl.BlockSpec((tf, d), lambda i, j: (j, 0)),
            pl.BlockSpec((1, d), lambda i, j: (0, 0)),
        ],
        out_specs=pl.BlockSpec((tm, d), lambda i, j: (i, 0)),
        out_shape=jax.ShapeDtypeStruct((t, d), F32),
        scratch_shapes=[pltpu.VMEM((tm, d), BF16)],
        compiler_params=pltpu.CompilerParams(
            dimension_semantics=("arbitrary", "arbitrary"), vmem_limit_bytes=VMEM_LIMIT),
        name="mlp",
    )(x1, mod, g, w1, w2, g_final)


def kernel(x, c, w_ada, b_ada, g_norm_mix, w_in, w_conv, b_conv, w_rg_a, b_rg_a, w_rg_x, b_rg_x,
           lru_lambda, g_attn_out, g_lru_out, w_out, g_norm_mlp, w_mlp_in, w_mlp_out, g_norm_final):
    batch, seq, d = x.shape
    assert w_ada.shape[0] == 1, "single layer: the final norm is fused into the MLP kernel"
    xt = x.reshape(batch * seq, d)
    c_pad = jnp.pad(c.astype(F32), ((0, 8 - batch), (0, 0)))
    mod_head, w_in_bf16 = _ada(c_pad, w_ada[0], b_ada, w_in[0], 2 * d)
    proj = _inproj(xt, mod_head[:batch].reshape(batch, 2, d), g_norm_mix, w_in_bf16, seq)
    o_attn, w1, wo, mod_rest = _attention(proj, w_mlp_in[0], w_out[0], c_pad, w_ada[0], b_ada, 2 * d,
                                          batch, seq)
    mod = jnp.concatenate([mod_head, mod_rest], axis=1)[:batch].reshape(batch, N_MOD, d)
    w_gates = jnp.concatenate([w_rg_a[0], w_rg_x[0]], axis=-1).astype(BF16)
    x1, w2 = _mix(proj, o_attn, xt, mod, w_conv[0], b_conv, w_gates, b_rg_a, b_rg_x, lru_lambda,
                  g_attn_out, g_lru_out, wo, w_mlp_out[0], seq)
    out = _mlp(x1, mod, g_norm_mlp, w1, w2, g_norm_final[None, :], seq)
    return out.reshape(batch, seq, d)
```

```python
import functools

import jax
import jax.numpy as jnp
from jax import lax
from jax.experimental import pallas as pl
from jax.experimental.pallas import tpu as pltpu

F32 = jnp.float32
BF16 = jnp.bfloat16

ATTN_HEADS = 8
HEAD_DIM = 128
LRU_HEADS = 8
LRU_BLOCK = 128
CONV_WIDTH = 4
LRU_C = 8.0
N_MOD = 6
EPS = 1e-6

LOG2_E = 1.4426950408889634
EXP2_UNDERFLOW = -151.0
MASKED_LOGIT = 1e30

SCORES_AHEAD = 2

V7X_VMEM_BYTES = 64 * 2 ** 20
VMEM_LANES = 128


def _nbytes(shape, dtype):
    lanes = -(-shape[-1] // VMEM_LANES) * VMEM_LANES
    n = lanes * jnp.dtype(dtype).itemsize
    for s in shape[:-1]:
        n *= s
    return n


def _vmem_limit(specs, dtypes, scratch=(), temporaries=()):
    need = sum(2 * _nbytes(s.block_shape, dt) for s, dt in zip(specs, dtypes))
    need += sum(_nbytes(s.shape, s.dtype) for s in scratch)
    need += sum(_nbytes(shape, dt) for shape, dt in temporaries)
    assert need <= V7X_VMEM_BYTES, need
    return need


def _call(body, name, grid, in_specs, operands, out_specs, out_shape, scratch=(), temporaries=()):
    dtypes = [o.dtype for o in operands] + [o.dtype for o in out_shape]
    return pl.pallas_call(
        body, grid=grid, in_specs=in_specs, out_specs=out_specs, out_shape=out_shape,
        scratch_shapes=list(scratch),
        compiler_params=pltpu.CompilerParams(
            dimension_semantics=("arbitrary",) * len(grid),
            vmem_limit_bytes=_vmem_limit(list(in_specs) + list(out_specs), dtypes, scratch, temporaries)),
        name=name,
    )(*operands)


def _softplus(x):
    return jnp.maximum(x, 0.0) + jnp.log1p(jnp.exp(-jnp.abs(x)))


def _rms_scale(x):
    return lax.rsqrt(jnp.mean(x * x, axis=-1, keepdims=True) + EPS)


def _modulation(c_ref, w_ref, b_ref):
    c = c_ref[...]
    c_act = c * jax.nn.sigmoid(c)
    return jnp.dot(c_act.astype(BF16), w_ref[...].astype(BF16),
                   preferred_element_type=F32) + b_ref[...]


def _ada_kernel(c_ref, w_ref, b_ref, winf_ref, o_ref, winb_ref):
    winb_ref[...] = winf_ref[...].astype(BF16)
    o_ref[...] = _modulation(c_ref, w_ref, b_ref)


def _ada(c_pad, w_ada, b_ada, w_in, n, tn=512):
    m, d = c_pad.shape
    steps = n // tn
    win_chunk = pl.BlockSpec((w_in.shape[0] // steps, w_in.shape[1]), lambda j: (j, 0))
    return _call(
        _ada_kernel, "ada", (steps,),
        in_specs=[
            pl.BlockSpec((m, d), lambda j: (0, 0)),
            pl.BlockSpec((d, tn), lambda j: (0, j)),
            pl.BlockSpec((1, tn), lambda j: (0, j)),
            win_chunk,
        ],
        operands=(c_pad, w_ada, b_ada, w_in),
        out_specs=[pl.BlockSpec((m, tn), lambda j: (0, j)), win_chunk],
        out_shape=[jax.ShapeDtypeStruct((m, n), F32), jax.ShapeDtypeStruct(w_in.shape, BF16)],
        temporaries=[((d, tn), BF16)],
    )


def _adaln(x, g, shift, scale):
    return ((x * _rms_scale(x) * g) * (1.0 + scale) + shift).astype(BF16)


def _inproj_kernel(x_ref, mod_ref, g_ref, w_ref, cs_ref, o_ref, h_ref, *, row_blocks):
    j = pl.program_id(1)
    rb = x_ref.shape[0] // row_blocks

    def project(h):
        return (jnp.dot(h, w_ref[...], preferred_element_type=F32) * cs_ref[...]).astype(BF16)

    @pl.when(j == 0)
    def _():
        for r in range(row_blocks):
            rows = slice(r * rb, (r + 1) * rb)
            h = _adaln(x_ref[rows, :], g_ref[...], mod_ref[0, 0:1, :], mod_ref[0, 1:2, :])
            h_ref[rows, :] = h
            o_ref[rows, :] = project(h)

    @pl.when(j > 0)
    def _():
        o_ref[...] = project(h_ref[...])


def _inproj(x2, mod_mix, g, w_in, seq, tm=1024, tn=1280):
    t, d = x2.shape
    n = w_in.shape[1]
    nt, nj = t // tm, n // tn
    col = lax.broadcasted_iota(jnp.int32, (1, n), 1)
    colscale = jnp.where(col < ATTN_HEADS * HEAD_DIM, HEAD_DIM ** -0.5 * LOG2_E, 1.0).astype(F32)
    return _call(
        functools.partial(_inproj_kernel, row_blocks=4), "inproj", (nt, nj),
        in_specs=[
            pl.BlockSpec((tm, d), lambda i, j: (i, 0)),
            pl.BlockSpec((1, 2, d), lambda i, j: (i * tm // seq, 0, 0)),
            pl.BlockSpec((1, d), lambda i, j: (0, 0)),
            pl.BlockSpec((d, tn), lambda i, j: (0, j)),
            pl.BlockSpec((1, tn), lambda i, j: (0, j)),
        ],
        operands=(x2, mod_mix, g, w_in, colscale),
        out_specs=[pl.BlockSpec((tm, tn), lambda i, j: (i, j))],
        out_shape=[jax.ShapeDtypeStruct((t, n), BF16)],
        scratch=[pltpu.VMEM((tm, d), BF16)],
        temporaries=[((tm, tn), F32)],
    )[0]


def _attn_kernel(q_ref, k_ref, v_ref, u_ref, w1f_ref, wof_ref, cond_ref, wa_ref, ba_ref,
                 o_ref, w1b_ref, wob_ref, modr_ref, c_ref, *, tq, nq):
    w1b_ref[...] = w1f_ref[...].astype(BF16)
    wob_ref[...] = wof_ref[...].astype(BF16)
    modr_ref[...] = _modulation(cond_ref, wa_ref, ba_ref)

    u = u_ref[...]
    rows = lax.broadcasted_iota(jnp.int32, (tq, tq), 0)
    cols = lax.broadcasted_iota(jnp.int32, (tq, tq), 1)
    causal = cols < rows

    def scores(qb, start, mask):
        k = k_ref[pl.ds(start, tq), :]
        z = lax.dot_general(qb, k, (((1,), (1,)), ((), ())), preferred_element_type=F32)
        if mask is not None:
            z = jnp.where(mask, z, -MASKED_LOGIT)
        t = jnp.log2(1.0 + jnp.exp2(-jnp.abs(z)))
        log_beta = jnp.minimum(z, 0.0) - t
        return log_beta, log_beta - z

    def weights(log_beta, log_stay, c):
        after = jnp.dot(log_stay.astype(BF16), u, preferred_element_type=F32)
        e = after + log_beta if c is None else after + (log_beta + c)
        return jnp.exp2(e).astype(BF16), after[:, 0:1] + log_stay[:, 0:1]

    pairs = [(i, j) for i in range(nq) for j in ((i, i - 1) if i > 0 else (i,))]
    carry = {}
    acc = {}

    def finish(i, j, lb, ls):
        v = v_ref[j * tq:(j + 1) * tq, :]
        if j == i:
            w, carry[i] = weights(lb, ls, None)
            acc[i] = jnp.dot(w, v, preferred_element_type=F32)
        else:
            w, rs = weights(lb, ls, carry[i])
            acc[i] = acc[i] + jnp.dot(w, v, preferred_element_type=F32)
            carry[i] = carry[i] + rs
        if j == max(i - 1, 0):
            o_ref[i * tq:(i + 1) * tq, :] = acc.pop(i)
            c = carry.pop(i)
            if i > 1:
                c_ref[i] = c

    pending = []
    for i, j in pairs:
        pending.append((i, j) + scores(q_ref[i * tq:(i + 1) * tq, :], j * tq,
                                       causal if j == i else None))
        if len(pending) > SCORES_AHEAD:
            finish(*pending.pop(0))
    for p in pending:
        finish(*p)

    if nq > 2:
        @pl.when(jnp.max(c_ref[2:nq]) > EXP2_UNDERFLOW)
        def _():
            for i in range(2, nq):
                lo_row = i * tq
                qb = q_ref[lo_row:lo_row + tq, :]

                def cond(state):
                    j, alive, _ = state
                    return jnp.logical_and(j >= 0, alive)

                def body(state):
                    j, _, c = state
                    start = pl.multiple_of(j * tq, tq)
                    lb, ls = scores(qb, start, None)
                    w, rs = weights(lb, ls, c)
                    o_ref[lo_row:lo_row + tq, :] += jnp.dot(
                        w, v_ref[pl.ds(start, tq), :], preferred_element_type=F32)
                    c = c + rs
                    return j - 1, jnp.max(c) > EXP2_UNDERFLOW, c

                c0 = c_ref[i]
                lax.while_loop(cond, body, (i - 2, jnp.max(c0) > EXP2_UNDERFLOW, c0))


def _attention(proj, w_mlp_in, w_out, c_pad, w_ada, b_ada, n_done, batch, seq, tq=256):
    t = proj.shape[0]
    nq = seq // tq
    r = lax.broadcasted_iota(jnp.int32, (tq, tq), 0)
    s = lax.broadcasted_iota(jnp.int32, (tq, tq), 1)
    u = (r > s).astype(BF16)
    h = ATTN_HEADS
    steps = batch * h
    d, f = w_mlp_in.shape
    w1_chunk = pl.BlockSpec((d, f // steps), lambda b, hh: (0, b * h + hh))
    wo_chunk = pl.BlockSpec((w_out.shape[0] // steps, w_out.shape[1]), lambda b, hh: (b * h + hh, 0))
    head = lambda off: pl.BlockSpec((seq, HEAD_DIM), lambda b, hh: (b, off + hh))
    m = c_pad.shape[0]
    n_rest = w_ada.shape[1] - n_done
    ta = n_rest // steps
    ada_chunk = lambda rows: pl.BlockSpec((rows, ta), lambda b, hh: (0, n_done // ta + b * h + hh))
    return _call(
        functools.partial(_attn_kernel, tq=tq, nq=nq), "attn", (batch, h),
        in_specs=[head(0), head(h), head(2 * h), pl.BlockSpec((tq, tq), lambda b, hh: (0, 0)),
                  w1_chunk, wo_chunk, pl.BlockSpec((m, d), lambda b, hh: (0, 0)),
                  ada_chunk(d), ada_chunk(1)],
        operands=(proj, proj, proj, u, w_mlp_in, w_out, c_pad, w_ada, b_ada),
        out_specs=[head(0), w1_chunk, wo_chunk,
                   pl.BlockSpec((m, ta), lambda b, hh: (0, b * h + hh))],
        out_shape=[jax.ShapeDtypeStruct((t, h * HEAD_DIM), F32),
                   jax.ShapeDtypeStruct(w_mlp_in.shape, BF16),
                   jax.ShapeDtypeStruct(w_out.shape, BF16),
                   jax.ShapeDtypeStruct((m, n_rest), F32)],
        scratch=[pltpu.VMEM((nq, tq, 1), F32)],
        temporaries=[((tq, tq), F32)] * (3 * (SCORES_AHEAD + 2)),
    )


def _mix_kernel(xr_ref, xg_ref, oa_ref, x_ref, mod_ref, wc_ref, bc_ref, wg_ref, ba_ref, bx_ref,
                lam_ref, ga_ref, gl_ref, wo_ref, w2f_ref, o_ref, w2b_ref,
                ext_ref, hc_ref, ol_ref, nl_ref, *, ts, ns):
    g = pl.program_id(0)
    hist = 8
    half = oa_ref.shape[1]

    w2b_ref[...] = w2f_ref[...].astype(BF16)

    @pl.when(g % ns == 0)
    def _():
        ext_ref[0:hist, :] = jnp.zeros((hist, ext_ref.shape[1]), F32)
        hc_ref[...] = jnp.zeros_like(hc_ref)

    @pl.when(g == 0)
    def _():
        nl_ref[...] = jnp.zeros_like(nl_ref)

    oa = oa_ref[...]
    na = (oa * _rms_scale(oa) * ga_ref[...]).astype(BF16)
    mixed = jnp.concatenate([na, nl_ref[(g + 1) % 2]], axis=1)
    ncol = o_ref.shape[1] // LRU_HEADS

    def project(h):
        cs = slice(h * ncol, (h + 1) * ncol)
        y = jnp.dot(mixed, wo_ref[:, cs], preferred_element_type=F32)
        o_ref[:, cs] = x_ref[:, cs] + mod_ref[0, 2:3, cs] * y

    ext_ref[hist:hist + ts, :] = xr_ref[...].astype(F32)
    nv = ts // 8
    sub = lax.broadcasted_iota(jnp.int32, (nv, 8, LRU_BLOCK), 1)

    def lru_head(h):
        sl = slice(h * LRU_BLOCK, (h + 1) * LRU_BLOCK)
        wc = wc_ref[:, sl]
        xe = ext_ref[0:hist + ts, sl].reshape(nv + 1, 8, LRU_BLOCK)
        xc = bc_ref[:, sl] + wc[CONV_WIDTH - 1:CONV_WIDTH, :] * xe[1:]
        for d in range(1, CONV_WIDTH):
            rolled = pltpu.roll(xe, d, 1)
            tap = CONV_WIDTH - 1 - d
            xc = xc + wc[tap:tap + 1, :] * jnp.where(sub >= d, rolled[1:], rolled[:-1])
        xc = xc.reshape(ts, LRU_BLOCK)
        gates = jnp.dot(xc.astype(BF16), wg_ref[h], preferred_element_type=F32)
        r = jax.nn.sigmoid(gates[:, :LRU_BLOCK] + ba_ref[:, sl])
        gi = jax.nn.sigmoid(gates[:, LRU_BLOCK:] + bx_ref[:, sl])
        log_a = (-LRU_C) * r * _softplus(-lam_ref[:, sl])
        a = jnp.exp(log_a)
        one_minus_a2 = -jnp.tanh(log_a) * (1.0 + a * a)
        b = jnp.exp2(jnp.log(one_minus_a2) * (0.5 * LOG2_E)) * (gi * xc)
        a = a.reshape(nv, 8, LRU_BLOCK)
        b = b.reshape(nv, 8, LRU_BLOCK)
        for d in (1, 2, 4):
            keep = sub >= d
            a_prev = pltpu.roll(a, d, 1)
            b_prev = pltpu.roll(b, d, 1)
            b = b + a * jnp.where(keep, b_prev, 0.0)
            a = a * jnp.where(keep, a_prev, 1.0)
        gate = jax.nn.gelu(xg_ref[:, sl].astype(F32), approximate=True).reshape(nv, 8, LRU_BLOCK)
        carry = hc_ref[0:1, sl]
        for i in range(nv):
            hcur = b[i] + a[i] * carry
            ol_ref[8 * i:8 * i + 8, sl] = hcur * gate[i]
            carry = hcur[7:8, :]
        hc_ref[0:1, sl] = carry

    for h in range(LRU_HEADS):
        project(h)
        lru_head(h)

    ext_ref[0:hist, :] = ext_ref[ts:ts + hist, :]
    ol = ol_ref[...]
    nl_ref[g % 2] = (ol * _rms_scale(ol) * gl_ref[...]).astype(BF16)


def _mix(proj, o_attn, x2, mod, w_conv, b_conv, w_gates, b_a, b_x, lam, g_a, g_l, w_out, w_mlp_out,
         seq, ts=256):
    t, d = x2.shape
    c = LRU_HEADS * LRU_BLOCK
    ns = seq // ts
    nblk = t // ts
    cur = lambda g: jnp.minimum(g, nblk - 1)
    prev = lambda g: jnp.maximum(g - 1, 0)
    vec = pl.BlockSpec((1, c), lambda g: (0, 0))
    xr_col = 3 * ATTN_HEADS * HEAD_DIM // c
    f = w_mlp_out.shape[0]
    w2_chunk = pl.BlockSpec((f // nblk, d), lambda g: (cur(g), 0))
    return _call(
        functools.partial(_mix_kernel, ts=ts, ns=ns), "mix", (nblk + 1,),
        in_specs=[
            pl.BlockSpec((ts, c), lambda g: (cur(g), xr_col)),
            pl.BlockSpec((ts, c), lambda g: (cur(g), xr_col + 1)),
            pl.BlockSpec((ts, c), lambda g: (prev(g), 0)),
            pl.BlockSpec((ts, d), lambda g: (prev(g), 0)),
            pl.BlockSpec((1, N_MOD, d), lambda g: (prev(g) // ns, 0, 0)),
            pl.BlockSpec((CONV_WIDTH, c), lambda g: (0, 0)),
            vec,
            pl.BlockSpec((LRU_HEADS, LRU_BLOCK, 2 * LRU_BLOCK), lambda g: (0, 0, 0)),
            vec, vec, vec, vec, vec,
            pl.BlockSpec((2 * c, d), lambda g: (0, 0)),
            w2_chunk,
        ],
        operands=(proj, proj, o_attn, x2, mod, w_conv, b_conv, w_gates, b_a, b_x, lam, g_a, g_l,
                  w_out, w_mlp_out),
        out_specs=[pl.BlockSpec((ts, d), lambda g: (prev(g), 0)), w2_chunk],
        out_shape=[jax.ShapeDtypeStruct((t, d), F32), jax.ShapeDtypeStruct(w_mlp_out.shape, BF16)],
        scratch=[pltpu.VMEM((ts + 8, c), F32), pltpu.VMEM((8, c), F32),
                 pltpu.VMEM((ts, c), F32), pltpu.VMEM((2, ts, c), BF16)],
        temporaries=[((ts, 2 * c), BF16), ((ts, c), F32), ((ts, c), F32)],
    )


def _mlp_kernel(x_ref, mod_ref, g_ref, w1_ref, w2_ref, gf_ref, o_ref, h_ref, *, row_blocks,
                mid_blocks):
    j = pl.program_id(1)
    last = pl.num_programs(1) - 1
    rb = x_ref.shape[0] // row_blocks

    def expand(h):
        hid = jnp.dot(h, w1_ref[...], preferred_element_type=F32)
        hid = jnp.square(jnp.maximum(hid, 0.0)).astype(BF16)
        return jnp.dot(hid, w2_ref[...], preferred_element_type=F32)

    @pl.when(j == 0)
    def _():
        for r in range(row_blocks):
            rows = slice(r * rb, (r + 1) * rb)
            h = _adaln(x_ref[rows, :], g_ref[...], mod_ref[0, 3:4, :], mod_ref[0, 4:5, :])
            h_ref[rows, :] = h
            o_ref[rows, :] = expand(h)

    @pl.when(jnp.logical_and(j > 0, j < last))
    def _():
        mb = x_ref.shape[0] // mid_blocks
        for r in range(mid_blocks):
            rows = slice(r * mb, (r + 1) * mb)
            o_ref[rows, :] += expand(h_ref[rows, :])

    @pl.when(j == last)
    def _():
        for r in range(row_blocks):
            rows = slice(r * rb, (r + 1) * rb)
            x2 = x_ref[rows, :] + mod_ref[0, 5:6, :] * (o_ref[rows, :] + expand(h_ref[rows, :]))
            o_ref[rows, :] = x2 * _rms_scale(x2) * gf_ref[...]


def _mlp(x1, mod, g, w1, w2, g_final, seq, tm=1024, tf=512):
    t, d = x1.shape
    f = w1.shape[1]
    assert f // tf >= 2, "first and last hidden-chunk steps are distinct code paths"
    row_blocks, mid_blocks = 4, 2
    return _call(
        functools.partial(_mlp_kernel, row_blocks=row_blocks, mid_blocks=mid_blocks), "mlp",
        (t // tm, f // tf),
        in_specs=[
            pl.BlockSpec((tm, d), lambda i, j: (i, 0)),
            pl.BlockSpec((1, N_MOD, d), lambda i, j: (i * tm // seq, 0, 0)),
            pl.BlockSpec((1, d), lambda i, j: (0, 0)),
            pl.BlockSpec((d, tf), lambda i, j: (0, j)),
            pl.BlockSpec((tf, d), lambda i, j: (j, 0)),
            pl.BlockSpec((1, d), lambda i, j: (0, 0)),
        ],
        operands=(x1, mod, g, w1, w2, g_final),
        out_specs=[pl.BlockSpec((tm, d), lambda i, j: (i, 0))],
        out_shape=[jax.ShapeDtypeStruct((t, d), F32)],
        scratch=[pltpu.VMEM((tm, d), BF16)],
        temporaries=[((tm // mid_blocks, tf), F32), ((tm // mid_blocks, tf), BF16),
                     ((tm // mid_blocks, d), F32)],
    )[0]


def kernel(x, c, w_ada, b_ada, g_norm_mix, w_in, w_conv, b_conv, w_rg_a, b_rg_a, w_rg_x, b_rg_x,
           lru_lambda, g_attn_out, g_lru_out, w_out, g_norm_mlp, w_mlp_in, w_mlp_out, g_norm_final):
    batch, seq, d = x.shape
    assert w_ada.shape[0] == 1, "single layer: the final norm is fused into the MLP kernel"
    xt = x.reshape(batch * seq, d)
    c_pad = jnp.pad(c.astype(F32), ((0, 8 - batch), (0, 0)))
    mod_head, w_in_bf16 = _ada(c_pad, w_ada[0], b_ada, w_in[0], 2 * d)
    proj = _inproj(xt, mod_head[:batch].reshape(batch, 2, d), g_norm_mix, w_in_bf16, seq)
    o_attn, w1, wo, mod_rest = _attention(proj, w_mlp_in[0], w_out[0], c_pad, w_ada[0], b_ada, 2 * d,
                                          batch, seq)
    mod = jnp.concatenate([mod_head, mod_rest], axis=1)[:batch].reshape(batch, N_MOD, d)
    w_gates = jnp.concatenate([w_rg_a[0], w_rg_x[0]], axis=-1).astype(BF16)
    x1, w2 = _mix(proj, o_attn, xt, mod, w_conv[0], b_conv, w_gates, b_rg_a, b_rg_x, lru_lambda,
                  g_attn_out, g_lru_out, wo, w_mlp_out[0], seq)
    out = _mlp(x1, mod, g_norm_mlp, w1, w2, g_norm_final[None, :], seq)
    return out.reshape(batch, seq, d)
```

```python
import functools

import jax
import jax.numpy as jnp
from jax import lax
from jax.experimental import pallas as pl
from jax.experimental.pallas import tpu as pltpu

F32 = jnp.float32
BF16 = jnp.bfloat16

ATTN_HEADS = 8
HEAD_DIM = 128
LRU_HEADS = 8
LRU_BLOCK = 128
CONV_WIDTH = 4
LRU_C = 8.0
N_MOD = 6
EPS = 1e-6

LOG2_E = 1.4426950408889634
EXP2_UNDERFLOW = -151.0
MASKED_LOGIT = 1e30

SCORES_AHEAD = 2

V7X_VMEM_BYTES = 64 * 2 ** 20
VMEM_LANES = 128


def _nbytes(shape, dtype):
    lanes = -(-shape[-1] // VMEM_LANES) * VMEM_LANES
    n = lanes * jnp.dtype(dtype).itemsize
    for s in shape[:-1]:
        n *= s
    return n


def _vmem_limit(specs, dtypes, scratch=(), temporaries=()):
    need = sum(2 * _nbytes(s.block_shape, dt) for s, dt in zip(specs, dtypes))
    need += sum(_nbytes(s.shape, s.dtype) for s in scratch)
    need += sum(_nbytes(shape, dt) for shape, dt in temporaries)
    assert need <= V7X_VMEM_BYTES, need
    return need


def _call(body, name, grid, in_specs, operands, out_specs, out_shape, scratch=(), temporaries=()):
    dtypes = [o.dtype for o in operands] + [o.dtype for o in out_shape]
    return pl.pallas_call(
        body, grid=grid, in_specs=in_specs, out_specs=out_specs, out_shape=out_shape,
        scratch_shapes=list(scratch),
        compiler_params=pltpu.CompilerParams(
            dimension_semantics=("arbitrary",) * len(grid),
            vmem_limit_bytes=_vmem_limit(list(in_specs) + list(out_specs), dtypes, scratch, temporaries)),
        name=name,
    )(*operands)


def _softplus(x):
    return jnp.maximum(x, 0.0) + jnp.log1p(jnp.exp(-jnp.abs(x)))


def _rms_scale(x):
    return lax.rsqrt(jnp.mean(x * x, axis=-1, keepdims=True) + EPS)


def _modulation(c_ref, w_ref, b_ref):
    c = c_ref[...]
    c_act = c * jax.nn.sigmoid(c)
    return jnp.dot(c_act.astype(BF16), w_ref[...].astype(BF16),
                   preferred_element_type=F32) + b_ref[...]


def _ada_kernel(c_ref, w_ref, b_ref, winf_ref, o_ref, winb_ref):
    winb_ref[...] = winf_ref[...].astype(BF16)
    o_ref[...] = _modulation(c_ref, w_ref, b_ref)


def _ada(c_pad, w_ada, b_ada, w_in, n, tn=512):
    m, d = c_pad.shape
    steps = n // tn
    win_chunk = pl.BlockSpec((w_in.shape[0] // steps, w_in.shape[1]), lambda j: (j, 0))
    return _call(
        _ada_kernel, "ada", (steps,),
        in_specs=[
            pl.BlockSpec((m, d), lambda j: (0, 0)),
            pl.BlockSpec((d, tn), lambda j: (0, j)),
            pl.BlockSpec((1, tn), lambda j: (0, j)),
            win_chunk,
        ],
        operands=(c_pad, w_ada, b_ada, w_in),
        out_specs=[pl.BlockSpec((m, tn), lambda j: (0, j)), win_chunk],
        out_shape=[jax.ShapeDtypeStruct((m, n), F32), jax.ShapeDtypeStruct(w_in.shape, BF16)],
        temporaries=[((d, tn), BF16)],
    )


def _adaln(x, g, shift, scale):
    return ((x * _rms_scale(x) * g) * (1.0 + scale) + shift).astype(BF16)


def _inproj_kernel(x_ref, mod_ref, g_ref, w_ref, cs_ref, o_ref, h_ref, *, row_blocks):
    j = pl.program_id(1)
    rb = x_ref.shape[0] // row_blocks

    def project(h):
        return (jnp.dot(h, w_ref[...], preferred_element_type=F32) * cs_ref[...]).astype(BF16)

    @pl.when(j == 0)
    def _():
        for r in range(row_blocks):
            rows = slice(r * rb, (r + 1) * rb)
            h = _adaln(x_ref[rows, :], g_ref[...], mod_ref[0, 0:1, :], mod_ref[0, 1:2, :])
            h_ref[rows, :] = h
            o_ref[rows, :] = project(h)

    @pl.when(j > 0)
    def _():
        o_ref[...] = project(h_ref[...])


def _inproj(x2, mod_mix, g, w_in, seq, tm=1024, tn=1280):
    t, d = x2.shape
    n = w_in.shape[1]
    nt, nj = t // tm, n // tn
    col = lax.broadcasted_iota(jnp.int32, (1, n), 1)
    colscale = jnp.where(col < ATTN_HEADS * HEAD_DIM, HEAD_DIM ** -0.5 * LOG2_E, 1.0).astype(F32)
    return _call(
        functools.partial(_inproj_kernel, row_blocks=4), "inproj", (nt, nj),
        in_specs=[
            pl.BlockSpec((tm, d), lambda i, j: (i, 0)),
            pl.BlockSpec((1, 2, d), lambda i, j: (i * tm // seq, 0, 0)),
            pl.BlockSpec((1, d), lambda i, j: (0, 0)),
            pl.BlockSpec((d, tn), lambda i, j: (0, j)),
            pl.BlockSpec((1, tn), lambda i, j: (0, j)),
        ],
        operands=(x2, mod_mix, g, w_in, colscale),
        out_specs=[pl.BlockSpec((tm, tn), lambda i, j: (i, j))],
        out_shape=[jax.ShapeDtypeStruct((t, n), BF16)],
        scratch=[pltpu.VMEM((tm, d), BF16)],
        temporaries=[((tm, tn), F32)],
    )[0]


def _attn_kernel(q_ref, k_ref, v_ref, u_ref, w1f_ref, wof_ref, cond_ref, wa_ref, ba_ref,
                 o_ref, w1b_ref, wob_ref, modr_ref, c_ref, *, tq, nq):
    w1b_ref[...] = w1f_ref[...].astype(BF16)
    wob_ref[...] = wof_ref[...].astype(BF16)
    modr_ref[...] = _modulation(cond_ref, wa_ref, ba_ref)

    u = u_ref[...]
    rows = lax.broadcasted_iota(jnp.int32, (tq, tq), 0)
    cols = lax.broadcasted_iota(jnp.int32, (tq, tq), 1)
    causal = cols < rows

    def scores(qb, start, mask):
        k = k_ref[pl.ds(start, tq), :]
        z = lax.dot_general(qb, k, (((1,), (1,)), ((), ())), preferred_element_type=F32)
        if mask is not None:
            z = jnp.where(mask, z, -MASKED_LOGIT)
        t = jnp.log2(1.0 + jnp.exp2(-jnp.abs(z)))
        log_beta = jnp.minimum(z, 0.0) - t
        return log_beta, log_beta - z

    def weights(log_beta, log_stay, c):
        after = jnp.dot(log_stay.astype(BF16), u, preferred_element_type=F32)
        e = after + log_beta if c is None else after + (log_beta + c)
        return jnp.exp2(e).astype(BF16), after[:, 0:1] + log_stay[:, 0:1]

    pairs = [(i, j) for i in range(nq) for j in ((i, i - 1) if i > 0 else (i,))]
    carry = {}
    acc = {}

    def finish(i, j, lb, ls):
        v = v_ref[j * tq:(j + 1) * tq, :]
        if j == i:
            w, carry[i] = weights(lb, ls, None)
            acc[i] = jnp.dot(w, v, preferred_element_type=F32)
        else:
            w, rs = weights(lb, ls, carry[i])
            acc[i] = acc[i] + jnp.dot(w, v, preferred_element_type=F32)
            carry[i] = carry[i] + rs
        if j == max(i - 1, 0):
            o_ref[i * tq:(i + 1) * tq, :] = acc.pop(i)
            c = carry.pop(i)
            if i > 1:
                c_ref[i] = c

    pending = []
    for i, j in pairs:
        pending.append((i, j) + scores(q_ref[i * tq:(i + 1) * tq, :], j * tq,
                                       causal if j == i else None))
        if len(pending) > SCORES_AHEAD:
            finish(*pending.pop(0))
    for p in pending:
        finish(*p)

    if nq > 2:
        @pl.when(jnp.max(c_ref[2:nq]) > EXP2_UNDERFLOW)
        def _():
            for i in range(2, nq):
                lo_row = i * tq
                qb = q_ref[lo_row:lo_row + tq, :]

                def cond(state):
                    j, alive, _ = state
                    return jnp.logical_and(j >= 0, alive)

                def body(state):
                    j, _, c = state
                    start = pl.multiple_of(j * tq, tq)
                    lb, ls = scores(qb, start, None)
                    w, rs = weights(lb, ls, c)
                    o_ref[lo_row:lo_row + tq, :] += jnp.dot(
                        w, v_ref[pl.ds(start, tq), :], preferred_element_type=F32)
                    c = c + rs
                    return j - 1, jnp.max(c) > EXP2_UNDERFLOW, c

                c0 = c_ref[i]
                lax.while_loop(cond, body, (i - 2, jnp.max(c0) > EXP2_UNDERFLOW, c0))


def _attention(proj, w_mlp_in, w_out, c_pad, w_ada, b_ada, n_done, batch, seq, tq=256):
    t = proj.shape[0]
    nq = seq // tq
    r = lax.broadcasted_iota(jnp.int32, (tq, tq), 0)
    s = lax.broadcasted_iota(jnp.int32, (tq, tq), 1)
    u = (r > s).astype(BF16)
    h = ATTN_HEADS
    steps = batch * h
    d, f = w_mlp_in.shape
    w1_chunk = pl.BlockSpec((d, f // steps), lambda b, hh: (0, b * h + hh))
    wo_chunk = pl.BlockSpec((w_out.shape[0] // steps, w_out.shape[1]), lambda b, hh: (b * h + hh, 0))
    head = lambda off: pl.BlockSpec((seq, HEAD_DIM), lambda b, hh: (b, off + hh))
    m = c_pad.shape[0]
    n_rest = w_ada.shape[1] - n_done
    ta = n_rest // steps
    ada_chunk = lambda rows: pl.BlockSpec((rows, ta), lambda b, hh: (0, n_done // ta + b * h + hh))
    return _call(
        functools.partial(_attn_kernel, tq=tq, nq=nq), "attn", (batch, h),
        in_specs=[head(0), head(h), head(2 * h), pl.BlockSpec((tq, tq), lambda b, hh: (0, 0)),
                  w1_chunk, wo_chunk, pl.BlockSpec((m, d), lambda b, hh: (0, 0)),
                  ada_chunk(d), ada_chunk(1)],
        operands=(proj, proj, proj, u, w_mlp_in, w_out, c_pad, w_ada, b_ada),
        out_specs=[head(0), w1_chunk, wo_chunk,
                   pl.BlockSpec((m, ta), lambda b, hh: (0, b * h + hh))],
        out_shape=[jax.ShapeDtypeStruct((t, h * HEAD_DIM), F32),
                   jax.ShapeDtypeStruct(w_mlp_in.shape, BF16),
                   jax.ShapeDtypeStruct(w_out.shape, BF16),
                   jax.ShapeDtypeStruct((m, n_rest), F32)],
        scratch=[pltpu.VMEM((nq, tq, 1), F32)],
        temporaries=[((tq, tq), F32)] * (3 * (SCORES_AHEAD + 2)),
    )


def _mix_kernel(xr_ref, xg_ref, oa_ref, x_ref, mod_ref, wc_ref, bc_ref, wg_ref, ba_ref, bx_ref,
                lam_ref, ga_ref, gl_ref, wo_ref, w2f_ref, o_ref, w2b_ref,
                ext_ref, hc_ref, ol_ref, nl_ref, *, ts, ns):
    g = pl.program_id(0)
    hist = 8
    half = oa_ref.shape[1]

    w2b_ref[...] = w2f_ref[...].astype(BF16)

    @pl.when(g % ns == 0)
    def _():
        ext_ref[0:hist, :] = jnp.zeros((hist, ext_ref.shape[1]), F32)
        hc_ref[...] = jnp.zeros_like(hc_ref)

    @pl.when(g == 0)
    def _():
        nl_ref[...] = jnp.zeros_like(nl_ref)

    oa = oa_ref[...]
    na = (oa * _rms_scale(oa) * ga_ref[...]).astype(BF16)
    mixed = jnp.concatenate([na, nl_ref[(g + 1) % 2]], axis=1)
    ncol = o_ref.shape[1] // LRU_HEADS

    def project(h):
        cs = slice(h * ncol, (h + 1) * ncol)
        y = jnp.dot(mixed, wo_ref[:, cs], preferred_element_type=F32)
        o_ref[:, cs] = x_ref[:, cs] + mod_ref[0, 2:3, cs] * y

    ext_ref[hist:hist + ts, :] = xr_ref[...].astype(F32)
    nv = ts // 8
    sub = lax.broadcasted_iota(jnp.int32, (nv, 8, LRU_BLOCK), 1)

    def lru_head(h):
        sl = slice(h * LRU_BLOCK, (h + 1) * LRU_BLOCK)
        wc = wc_ref[:, sl]
        xe = ext_ref[0:hist + ts, sl].reshape(nv + 1, 8, LRU_BLOCK)
        xc = bc_ref[:, sl] + wc[CONV_WIDTH - 1:CONV_WIDTH, :] * xe[1:]
        for d in range(1, CONV_WIDTH):
            rolled = pltpu.roll(xe, d, 1)
            tap = CONV_WIDTH - 1 - d
            xc = xc + wc[tap:tap + 1, :] * jnp.where(sub >= d, rolled[1:], rolled[:-1])
        xc = xc.reshape(ts, LRU_BLOCK)
        gates = jnp.dot(xc.astype(BF16), wg_ref[h], preferred_element_type=F32)
        r = jax.nn.sigmoid(gates[:, :LRU_BLOCK] + ba_ref[:, sl])
        gi = jax.nn.sigmoid(gates[:, LRU_BLOCK:] + bx_ref[:, sl])
        log_a = (-LRU_C) * r * _softplus(-lam_ref[:, sl])
        a = jnp.exp(log_a)
        one_minus_a2 = -jnp.tanh(log_a) * (1.0 + a * a)
        b = jnp.exp2(jnp.log(one_minus_a2) * (0.5 * LOG2_E)) * (gi * xc)
        a = a.reshape(nv, 8, LRU_BLOCK)
        b = b.reshape(nv, 8, LRU_BLOCK)
        for d in (1, 2, 4):
            keep = sub >= d
            a_prev = pltpu.roll(a, d, 1)
            b_prev = pltpu.roll(b, d, 1)
            b = b + a * jnp.where(keep, b_prev, 0.0)
            a = a * jnp.where(keep, a_prev, 1.0)
        gate = jax.nn.gelu(xg_ref[:, sl].astype(F32), approximate=True).reshape(nv, 8, LRU_BLOCK)
        carry = hc_ref[0:1, sl]
        for i in range(nv):
            hcur = b[i] + a[i] * carry
            ol_ref[8 * i:8 * i + 8, sl] = hcur * gate[i]
            carry = hcur[7:8, :]
        hc_ref[0:1, sl] = carry

    for h in range(LRU_HEADS):
        project(h)
        lru_head(h)

    ext_ref[0:hist, :] = ext_ref[ts:ts + hist, :]
    ol = ol_ref[...]
    nl_ref[g % 2] = (ol * _rms_scale(ol) * gl_ref[...]).astype(BF16)


def _mix(proj, o_attn, x2, mod, w_conv, b_conv, w_gates, b_a, b_x, lam, g_a, g_l, w_out, w_mlp_out,
         seq, ts=256):
    t, d = x2.shape
    c = LRU_HEADS * LRU_BLOCK
    ns = seq // ts
    nblk = t // ts
    cur = lambda g: jnp.minimum(g, nblk - 1)
    prev = lambda g: jnp.maximum(g - 1, 0)
    vec = pl.BlockSpec((1, c), lambda g: (0, 0))
    xr_col = 3 * ATTN_HEADS * HEAD_DIM // c
    f = w_mlp_out.shape[0]
    w2_chunk = pl.BlockSpec((f // nblk, d), lambda g: (cur(g), 0))
    return _call(
        functools.partial(_mix_kernel, ts=ts, ns=ns), "mix", (nblk + 1,),
        in_specs=[
            pl.BlockSpec((ts, c), lambda g: (cur(g), xr_col)),
            pl.BlockSpec((ts, c), lambda g: (cur(g), xr_col + 1)),
            pl.BlockSpec((ts, c), lambda g: (prev(g), 0)),
            pl.BlockSpec((ts, d), lambda g: (prev(g), 0)),
            pl.BlockSpec((1, N_MOD, d), lambda g: (prev(g) // ns, 0, 0)),
            pl.BlockSpec((CONV_WIDTH, c), lambda g: (0, 0)),
            vec,
            pl.BlockSpec((LRU_HEADS, LRU_BLOCK, 2 * LRU_BLOCK), lambda g: (0, 0, 0)),
            vec, vec, vec, vec, vec,
            pl.BlockSpec((2 * c, d), lambda g: (0, 0)),
            w2_chunk,
        ],
        operands=(proj, proj, o_attn, x2, mod, w_conv, b_conv, w_gates, b_a, b_x, lam, g_a, g_l,
                  w_out, w_mlp_out),
        out_specs=[pl.BlockSpec((ts, d), lambda g: (prev(g), 0)), w2_chunk],
        out_shape=[jax.ShapeDtypeStruct((t, d), F32), jax.ShapeDtypeStruct(w_mlp_out.shape, BF16)],
        scratch=[pltpu.VMEM((ts + 8, c), F32), pltpu.VMEM((8, c), F32),
                 pltpu.VMEM((ts, c), F32), pltpu.VMEM((2, ts, c), BF16)],
        temporaries=[((ts, 2 * c), BF16), ((ts, c), F32), ((ts, c), F32)],
    )


def _mlp_kernel(x_ref, mod_ref, g_ref, w1_ref, w2_ref, gf_ref, o_ref, h_ref, *, row_blocks,
                mid_blocks):
    j = pl.program_id(1)
    last = pl.num_programs(1) - 1
    rb = x_ref.shape[0] // row_blocks

    def expand(h):
        hid = jnp.dot(h, w1_ref[...], preferred_element_type=F32)
        hid = jnp.square(jnp.maximum(hid, 0.0)).astype(BF16)
        return jnp.dot(hid, w2_ref[...], preferred_element_type=F32)

    @pl.when(j == 0)
    def _():
        for r in range(row_blocks):
            rows = slice(r * rb, (r + 1) * rb)
            h = _adaln(x_ref[rows, :], g_ref[...], mod_ref[0, 3:4, :], mod_ref[0, 4:5, :])
            h_ref[rows, :] = h
            o_ref[rows, :] = expand(h)

    @pl.when(jnp.logical_and(j > 0, j < last))
    def _():
        mb = x_ref.shape[0] // mid_blocks
        for r in range(mid_blocks):
            rows = slice(r * mb, (r + 1) * mb)
            o_ref[rows, :] += expand(h_ref[rows, :])

    @pl.when(j == last)
    def _():
        for r in range(row_blocks):
            rows = slice(r * rb, (r + 1) * rb)
            x2 = x_ref[rows, :] + mod_ref[0, 5:6, :] * (o_ref[rows, :] + expand(h_ref[rows, :]))
            o_ref[rows, :] = x2 * _rms_scale(x2) * gf_ref[...]


def _mlp(x1, mod, g, w1, w2, g_final, seq, tm=512, tf=1024):
    t, d = x1.shape
    f = w1.shape[1]
    assert f // tf >= 2, "first and last hidden-chunk steps are distinct code paths"
    row_blocks, mid_blocks = 2, 1
    return _call(
        functools.partial(_mlp_kernel, row_blocks=row_blocks, mid_blocks=mid_blocks), "mlp",
        (t // tm, f // tf),
        in_specs=[
            pl.BlockSpec((tm, d), lambda i, j: (i, 0)),
            pl.BlockSpec((1, N_MOD, d), lambda i, j: (i * tm // seq, 0, 0)),
            pl.BlockSpec((1, d), lambda i, j: (0, 0)),
            pl.BlockSpec((d, tf), lambda i, j: (0, j)),
            pl.BlockSpec((tf, d), lambda i, j: (j, 0)),
            pl.BlockSpec((1, d), lambda i, j: (0, 0)),
        ],
        operands=(x1, mod, g, w1, w2, g_final),
        out_specs=[pl.BlockSpec((tm, d), lambda i, j: (i, 0))],
        out_shape=[jax.ShapeDtypeStruct((t, d), F32)],
        scratch=[pltpu.VMEM((tm, d), BF16)],
        temporaries=[((tm // mid_blocks, tf), F32), ((tm // mid_blocks, tf), BF16),
                     ((tm // mid_blocks, d), F32)],
    )[0]


def kernel(x, c, w_ada, b_ada, g_norm_mix, w_in, w_conv, b_conv, w_rg_a, b_rg_a, w_rg_x, b_rg_x,
           lru_lambda, g_attn_out, g_lru_out, w_out, g_norm_mlp, w_mlp_in, w_mlp_out, g_norm_final):
    batch, seq, d = x.shape
    assert w_ada.shape[0] == 1, "single layer: the final norm is fused into the MLP kernel"
    xt = x.reshape(batch * seq, d)
    c_pad = jnp.pad(c.astype(F32), ((0, 8 - batch), (0, 0)))
    mod_head, w_in_bf16 = _ada(c_pad, w_ada[0], b_ada, w_in[0], 2 * d)
    proj = _inproj(xt, mod_head[:batch].reshape(batch, 2, d), g_norm_mix, w_in_bf16, seq)
    o_attn, w1, wo, mod_rest = _attention(proj, w_mlp_in[0], w_out[0], c_pad, w_ada[0], b_ada, 2 * d,
                                          batch, seq)
    mod = jnp.concatenate([mod_head, mod_rest], axis=1)[:batch].reshape(batch, N_MOD, d)
    w_gates = jnp.concatenate([w_rg_a[0], w_rg_x[0]], axis=-1).astype(BF16)
    x1, w2 = _mix(proj, o_attn, xt, mod, w_conv[0], b_conv, w_gates, b_rg_a, b_rg_x, lru_lambda,
                  g_attn_out, g_lru_out, wo, w_mlp_out[0], seq)
    out = _mlp(x1, mod, g_norm_mlp, w1, w2, g_norm_final[None, :], seq)
    return out.reshape(batch, seq, d)
```

```python
import functools

import jax
import jax.numpy as jnp
from jax import lax
from jax.experimental import pallas as pl
from jax.experimental.pallas import tpu as pltpu

F32 = jnp.float32
BF16 = jnp.bfloat16

ATTN_HEADS = 8
HEAD_DIM = 128
LRU_HEADS = 8
LRU_BLOCK = 128
CONV_WIDTH = 4
LRU_C = 8.0
N_MOD = 6
N_MOD_HEAD = 2
EPS = 1e-6

LOG2_E = 1.4426950408889634
EXP2_UNDERFLOW = -151.0
MASKED_LOGIT = 1e30

SCORES_AHEAD = 2

V7X_VMEM_BYTES = 64 * 2 ** 20
VMEM_LANES = 128


def _nbytes(shape, dtype):
    lanes = -(-shape[-1] // VMEM_LANES) * VMEM_LANES
    n = lanes * jnp.dtype(dtype).itemsize
    for s in shape[:-1]:
        n *= s
    return n


def _vmem_limit(specs, dtypes, scratch=(), temporaries=()):
    need = sum(2 * _nbytes(s.block_shape, dt) for s, dt in zip(specs, dtypes))
    need += sum(_nbytes(s.shape, s.dtype) for s in scratch)
    need += sum(_nbytes(shape, dt) for shape, dt in temporaries)
    assert need <= V7X_VMEM_BYTES, need
    return need


def _call(body, name, grid, in_specs, operands, out_specs, out_shape, scratch=(), temporaries=()):
    dtypes = [o.dtype for o in operands] + [o.dtype for o in out_shape]
    return pl.pallas_call(
        body, grid=grid, in_specs=in_specs, out_specs=out_specs, out_shape=out_shape,
        scratch_shapes=list(scratch),
        compiler_params=pltpu.CompilerParams(
            dimension_semantics=("arbitrary",) * len(grid),
            vmem_limit_bytes=_vmem_limit(list(in_specs) + list(out_specs), dtypes, scratch, temporaries)),
        name=name,
    )(*operands)


def _softplus(x):
    return jnp.maximum(x, 0.0) + jnp.log1p(jnp.exp(-jnp.abs(x)))


def _rms_scale(x):
    return lax.rsqrt(jnp.mean(x * x, axis=-1, keepdims=True) + EPS)


def _modulation(c_ref, w_ref, b_ref):
    c = c_ref[...]
    c_act = c * jax.nn.sigmoid(c)
    return jnp.dot(c_act.astype(BF16), w_ref[...].astype(BF16),
                   preferred_element_type=F32) + b_ref[...]


def _ada_kernel(c_ref, w_ref, b_ref, winf_ref, o_ref, winb_ref):
    winb_ref[...] = winf_ref[...].astype(BF16)
    o_ref[...] = _modulation(c_ref, w_ref, b_ref)


def _ada(c_pad, w_ada, b_ada, w_in, n, tn=512):
    m, d = c_pad.shape
    steps = n // tn
    win_chunk = pl.BlockSpec((w_in.shape[0] // steps, w_in.shape[1]), lambda j: (j, 0))
    return _call(
        _ada_kernel, "ada", (steps,),
        in_specs=[
            pl.BlockSpec((m, d), lambda j: (0, 0)),
            pl.BlockSpec((d, tn), lambda j: (0, j)),
            pl.BlockSpec((1, tn), lambda j: (0, j)),
            win_chunk,
        ],
        operands=(c_pad, w_ada, b_ada, w_in),
        out_specs=[pl.BlockSpec((m, tn), lambda j: (0, j)), win_chunk],
        out_shape=[jax.ShapeDtypeStruct((m, n), F32), jax.ShapeDtypeStruct(w_in.shape, BF16)],
        temporaries=[((d, tn), BF16)],
    )


def _adaln(x, g, shift, scale):
    return ((x * _rms_scale(x) * g) * (1.0 + scale) + shift).astype(BF16)


def _inproj_kernel(x_ref, mod_ref, g_ref, w_ref, cs_ref, o_ref, h_ref, *, row_blocks):
    j = pl.program_id(1)
    rb = x_ref.shape[0] // row_blocks

    def project(h):
        return (jnp.dot(h, w_ref[...], preferred_element_type=F32) * cs_ref[...]).astype(BF16)

    @pl.when(j == 0)
    def _():
        for r in range(row_blocks):
            rows = slice(r * rb, (r + 1) * rb)
            h = _adaln(x_ref[rows, :], g_ref[...], mod_ref[0, 0:1, :], mod_ref[0, 1:2, :])
            h_ref[rows, :] = h
            o_ref[rows, :] = project(h)

    @pl.when(j > 0)
    def _():
        o_ref[...] = project(h_ref[...])


def _inproj(x2, mod_mix, g, w_in, seq, tm=1024, tn=1280):
    t, d = x2.shape
    n = w_in.shape[1]
    nt, nj = t // tm, n // tn
    col = lax.broadcasted_iota(jnp.int32, (1, n), 1)
    colscale = jnp.where(col < ATTN_HEADS * HEAD_DIM, HEAD_DIM ** -0.5 * LOG2_E, 1.0).astype(F32)
    return _call(
        functools.partial(_inproj_kernel, row_blocks=4), "inproj", (nt, nj),
        in_specs=[
            pl.BlockSpec((tm, d), lambda i, j: (i, 0)),
            pl.BlockSpec((1, 2, d), lambda i, j: (i * tm // seq, 0, 0)),
            pl.BlockSpec((1, d), lambda i, j: (0, 0)),
            pl.BlockSpec((d, tn), lambda i, j: (0, j)),
            pl.BlockSpec((1, tn), lambda i, j: (0, j)),
        ],
        operands=(x2, mod_mix, g, w_in, colscale),
        out_specs=[pl.BlockSpec((tm, tn), lambda i, j: (i, j))],
        out_shape=[jax.ShapeDtypeStruct((t, n), BF16)],
        scratch=[pltpu.VMEM((tm, d), BF16)],
        temporaries=[((tm, tn), F32)],
    )[0]


def _attn_kernel(q_ref, k_ref, v_ref, u_ref, w1f_ref, wof_ref, cond_ref, wa_ref, ba_ref,
                 o_ref, w1b_ref, wob_ref, modr_ref, c_ref, *, tq, nq):
    w1b_ref[...] = w1f_ref[...].astype(BF16)
    wob_ref[...] = wof_ref[...].astype(BF16)
    modr_ref[...] = _modulation(cond_ref, wa_ref, ba_ref)

    u = u_ref[...]
    rows = lax.broadcasted_iota(jnp.int32, (tq, tq), 0)
    cols = lax.broadcasted_iota(jnp.int32, (tq, tq), 1)
    causal = cols < rows

    hps = q_ref.shape[1] // HEAD_DIM
    lanes = lambda hd: slice(hd * HEAD_DIM, (hd + 1) * HEAD_DIM)

    def scores(hd, qb, start, mask):
        k = k_ref[pl.ds(start, tq), lanes(hd)]
        z = lax.dot_general(qb, k, (((1,), (1,)), ((), ())), preferred_element_type=F32)
        if mask is not None:
            z = jnp.where(mask, z, -MASKED_LOGIT)
        t = jnp.log2(1.0 + jnp.exp2(-jnp.abs(z)))
        log_beta = jnp.minimum(z, 0.0) - t
        return log_beta, log_beta - z

    def weights(log_beta, log_stay, c):
        after = jnp.dot(log_stay.astype(BF16), u, preferred_element_type=F32)
        e = after + log_beta if c is None else after + (log_beta + c)
        return jnp.exp2(e).astype(BF16), after[:, 0:1] + log_stay[:, 0:1]

    pairs = [(hd, i, j) for i in range(nq) for hd in range(hps)
             for j in ((i, i - 1) if i > 0 else (i,))]
    carry = {}
    acc = {}

    def finish(hd, i, j, lb, ls):
        v = v_ref[j * tq:(j + 1) * tq, lanes(hd)]
        if j == i:
            w, carry[hd, i] = weights(lb, ls, None)
            acc[hd, i] = jnp.dot(w, v, preferred_element_type=F32)
        else:
            w, rs = weights(lb, ls, carry[hd, i])
            acc[hd, i] = acc[hd, i] + jnp.dot(w, v, preferred_element_type=F32)
            carry[hd, i] = carry[hd, i] + rs
        if j == max(i - 1, 0):
            o_ref[i * tq:(i + 1) * tq, lanes(hd)] = acc.pop((hd, i))
            c = carry.pop((hd, i))
            if i > 1:
                c_ref[hd, i] = c

    pending = []
    for hd, i, j in pairs:
        pending.append((hd, i, j) + scores(hd, q_ref[i * tq:(i + 1) * tq, lanes(hd)], j * tq,
                                           causal if j == i else None))
        if len(pending) > SCORES_AHEAD:
            finish(*pending.pop(0))
    for p in pending:
        finish(*p)

    if nq > 2:
        @pl.when(jnp.max(c_ref[:, 2:nq]) > EXP2_UNDERFLOW)
        def _():
            for hd, i in [(hd, i) for hd in range(hps) for i in range(2, nq)]:
                lo_row = i * tq
                qb = q_ref[lo_row:lo_row + tq, lanes(hd)]

                def cond(state):
                    j, alive, _ = state
                    return jnp.logical_and(j >= 0, alive)

                def body(state, hd=hd, lo_row=lo_row, qb=qb):
                    j, _, c = state
                    start = pl.multiple_of(j * tq, tq)
                    lb, ls = scores(hd, qb, start, None)
                    w, rs = weights(lb, ls, c)
                    o_ref[lo_row:lo_row + tq, lanes(hd)] += jnp.dot(
                        w, v_ref[pl.ds(start, tq), lanes(hd)], preferred_element_type=F32)
                    c = c + rs
                    return j - 1, jnp.max(c) > EXP2_UNDERFLOW, c

                c0 = c_ref[hd, i]
                lax.while_loop(cond, body, (i - 2, jnp.max(c0) > EXP2_UNDERFLOW, c0))


def _attention(proj, w_mlp_in, w_out, c_pad, w_ada, b_ada, n_done, batch, seq, tq=256, hps=2):
    t = proj.shape[0]
    nq = seq // tq
    r = lax.broadcasted_iota(jnp.int32, (tq, tq), 0)
    s = lax.broadcasted_iota(jnp.int32, (tq, tq), 1)
    u = (r > s).astype(BF16)
    h = ATTN_HEADS // hps
    steps = batch * h
    d, f = w_mlp_in.shape
    w1_chunk = pl.BlockSpec((d, f // steps), lambda b, hh: (0, b * h + hh))
    wo_chunk = pl.BlockSpec((w_out.shape[0] // steps, w_out.shape[1]), lambda b, hh: (b * h + hh, 0))
    head = lambda off: pl.BlockSpec((seq, hps * HEAD_DIM), lambda b, hh: (b, off + hh))
    m = c_pad.shape[0]
    n_rest = w_ada.shape[1] - n_done
    ta = n_rest // steps
    ada_chunk = lambda rows: pl.BlockSpec((rows, ta), lambda b, hh: (0, n_done // ta + b * h + hh))
    return _call(
        functools.partial(_attn_kernel, tq=tq, nq=nq), "attn", (batch, h),
        in_specs=[head(0), head(h), head(2 * h), pl.BlockSpec((tq, tq), lambda b, hh: (0, 0)),
                  w1_chunk, wo_chunk, pl.BlockSpec((m, d), lambda b, hh: (0, 0)),
                  ada_chunk(d), ada_chunk(1)],
        operands=(proj, proj, proj, u, w_mlp_in, w_out, c_pad, w_ada, b_ada),
        out_specs=[head(0), w1_chunk, wo_chunk,
                   pl.BlockSpec((m, ta), lambda b, hh: (0, b * h + hh))],
        out_shape=[jax.ShapeDtypeStruct((t, ATTN_HEADS * HEAD_DIM), F32),
                   jax.ShapeDtypeStruct(w_mlp_in.shape, BF16),
                   jax.ShapeDtypeStruct(w_out.shape, BF16),
                   jax.ShapeDtypeStruct((m, n_rest), F32)],
        scratch=[pltpu.VMEM((hps, nq, tq, 1), F32)],
        temporaries=[((tq, tq), F32)] * (3 * (SCORES_AHEAD + 2)),
    )


def _mix_kernel(xr_ref, xg_ref, oa_ref, x_ref, mod_ref, wc_ref, bc_ref, wg_ref, ba_ref, bx_ref,
                lam_ref, ga_ref, gl_ref, wo_ref, w2f_ref, o_ref, w2b_ref,
                ext_ref, hc_ref, ol_ref, nl_ref, *, ts, ns):
    g = pl.program_id(0)
    hist = 8
    half = oa_ref.shape[1]

    w2b_ref[...] = w2f_ref[...].astype(BF16)

    @pl.when(g % ns == 0)
    def _():
        ext_ref[0:hist, :] = jnp.zeros((hist, ext_ref.shape[1]), F32)
        hc_ref[...] = jnp.zeros_like(hc_ref)

    @pl.when(g == 0)
    def _():
        nl_ref[...] = jnp.zeros_like(nl_ref)

    oa = oa_ref[...]
    na = (oa * _rms_scale(oa) * ga_ref[...]).astype(BF16)
    mixed = jnp.concatenate([na, nl_ref[(g + 1) % 2]], axis=1)
    ncol = o_ref.shape[1] // LRU_HEADS

    def project(h):
        cs = slice(h * ncol, (h + 1) * ncol)
        y = jnp.dot(mixed, wo_ref[:, cs], preferred_element_type=F32)
        o_ref[:, cs] = x_ref[:, cs] + mod_ref[0, 0:1, cs] * y

    ext_ref[hist:hist + ts, :] = xr_ref[...].astype(F32)
    nv = ts // 8
    sub = lax.broadcasted_iota(jnp.int32, (nv, 8, LRU_BLOCK), 1)

    def lru_head(h):
        sl = slice(h * LRU_BLOCK, (h + 1) * LRU_BLOCK)
        wc = wc_ref[:, sl]
        xe = ext_ref[0:hist + ts, sl].reshape(nv + 1, 8, LRU_BLOCK)
        xc = bc_ref[:, sl] + wc[CONV_WIDTH - 1:CONV_WIDTH, :] * xe[1:]
        for d in range(1, CONV_WIDTH):
            rolled = pltpu.roll(xe, d, 1)
            tap = CONV_WIDTH - 1 - d
            xc = xc + wc[tap:tap + 1, :] * jnp.where(sub >= d, rolled[1:], rolled[:-1])
        xc = xc.reshape(ts, LRU_BLOCK)
        gates = jnp.dot(xc.astype(BF16), wg_ref[h], preferred_element_type=F32)
        r = jax.nn.sigmoid(gates[:, :LRU_BLOCK] + ba_ref[:, sl])
        gi = jax.nn.sigmoid(gates[:, LRU_BLOCK:] + bx_ref[:, sl])
        log_a = (-LRU_C) * r * _softplus(-lam_ref[:, sl])
        a = jnp.exp(log_a)
        one_minus_a2 = -jnp.tanh(log_a) * (1.0 + a * a)
        b = jnp.exp2(jnp.log(one_minus_a2) * (0.5 * LOG2_E)) * (gi * xc)
        a = a.reshape(nv, 8, LRU_BLOCK)
        b = b.reshape(nv, 8, LRU_BLOCK)
        for d in (1, 2, 4):
            keep = sub >= d
            a_prev = pltpu.roll(a, d, 1)
            b_prev = pltpu.roll(b, d, 1)
            b = b + a * jnp.where(keep, b_prev, 0.0)
            a = a * jnp.where(keep, a_prev, 1.0)
        gate = jax.nn.gelu(xg_ref[:, sl].astype(F32), approximate=True).reshape(nv, 8, LRU_BLOCK)
        carry = hc_ref[0:1, sl]
        for i in range(nv):
            hcur = b[i] + a[i] * carry
            ol_ref[8 * i:8 * i + 8, sl] = hcur * gate[i]
            carry = hcur[7:8, :]
        hc_ref[0:1, sl] = carry

    for h in range(LRU_HEADS):
        project(h)
        lru_head(h)

    ext_ref[0:hist, :] = ext_ref[ts:ts + hist, :]
    ol = ol_ref[...]
    nl_ref[g % 2] = (ol * _rms_scale(ol) * gl_ref[...]).astype(BF16)


def _mix(proj, o_attn, x2, mod, w_conv, b_conv, w_gates, b_a, b_x, lam, g_a, g_l, w_out, w_mlp_out,
         seq, ts=256):
    t, d = x2.shape
    c = LRU_HEADS * LRU_BLOCK
    ns = seq // ts
    nblk = t // ts
    cur = lambda g: jnp.minimum(g, nblk - 1)
    prev = lambda g: jnp.maximum(g - 1, 0)
    vec = pl.BlockSpec((1, c), lambda g: (0, 0))
    xr_col = 3 * ATTN_HEADS * HEAD_DIM // c
    f = w_mlp_out.shape[0]
    w2_chunk = pl.BlockSpec((f // nblk, d), lambda g: (cur(g), 0))
    return _call(
        functools.partial(_mix_kernel, ts=ts, ns=ns), "mix", (nblk + 1,),
        in_specs=[
            pl.BlockSpec((ts, c), lambda g: (cur(g), xr_col)),
            pl.BlockSpec((ts, c), lambda g: (cur(g), xr_col + 1)),
            pl.BlockSpec((ts, c), lambda g: (prev(g), 0)),
            pl.BlockSpec((ts, d), lambda g: (prev(g), 0)),
            pl.BlockSpec((1, N_MOD - N_MOD_HEAD, d), lambda g: (prev(g) // ns, 0, 0)),
            pl.BlockSpec((CONV_WIDTH, c), lambda g: (0, 0)),
            vec,
            pl.BlockSpec((LRU_HEADS, LRU_BLOCK, 2 * LRU_BLOCK), lambda g: (0, 0, 0)),
            vec, vec, vec, vec, vec,
            pl.BlockSpec((2 * c, d), lambda g: (0, 0)),
            w2_chunk,
        ],
        operands=(proj, proj, o_attn, x2, mod, w_conv, b_conv, w_gates, b_a, b_x, lam, g_a, g_l,
                  w_out, w_mlp_out),
        out_specs=[pl.BlockSpec((ts, d), lambda g: (prev(g), 0)), w2_chunk],
        out_shape=[jax.ShapeDtypeStruct((t, d), F32), jax.ShapeDtypeStruct(w_mlp_out.shape, BF16)],
        scratch=[pltpu.VMEM((ts + 8, c), F32), pltpu.VMEM((8, c), F32),
                 pltpu.VMEM((ts, c), F32), pltpu.VMEM((2, ts, c), BF16)],
        temporaries=[((ts, 2 * c), BF16), ((ts, c), F32), ((ts, c), F32)],
    )


def _mlp_kernel(x_ref, mod_ref, g_ref, w1_ref, w2_ref, gf_ref, o_ref, h_ref, *, row_blocks,
                mid_blocks):
    j = pl.program_id(1)
    last = pl.num_programs(1) - 1
    rb = x_ref.shape[0] // row_blocks

    def expand(h):
        hid = jnp.dot(h, w1_ref[...], preferred_element_type=F32)
        hid = jnp.square(jnp.maximum(hid, 0.0)).astype(BF16)
        return jnp.dot(hid, w2_ref[...], preferred_element_type=F32)

    @pl.when(j == 0)
    def _():
        for r in range(row_blocks):
            rows = slice(r * rb, (r + 1) * rb)
            h = _adaln(x_ref[rows, :], g_ref[...], mod_ref[0, 1:2, :], mod_ref[0, 2:3, :])
            h_ref[rows, :] = h
            o_ref[rows, :] = expand(h)

    @pl.when(jnp.logical_and(j > 0, j < last))
    def _():
        mb = x_ref.shape[0] // mid_blocks
        for r in range(mid_blocks):
            rows = slice(r * mb, (r + 1) * mb)
            o_ref[rows, :] += expand(h_ref[rows, :])

    @pl.when(j == last)
    def _():
        for r in range(row_blocks):
            rows = slice(r * rb, (r + 1) * rb)
            x2 = x_ref[rows, :] + mod_ref[0, 3:4, :] * (o_ref[rows, :] + expand(h_ref[rows, :]))
            o_ref[rows, :] = x2 * _rms_scale(x2) * gf_ref[...]


def _mlp(x1, mod, g, w1, w2, g_final, seq, tm=512, tf=1024):
    t, d = x1.shape
    f = w1.shape[1]
    assert f // tf >= 2, "first and last hidden-chunk steps are distinct code paths"
    row_blocks, mid_blocks = 2, 1
    return _call(
        functools.partial(_mlp_kernel, row_blocks=row_blocks, mid_blocks=mid_blocks), "mlp",
        (t // tm, f // tf),
        in_specs=[
            pl.BlockSpec((tm, d), lambda i, j: (i, 0)),
            pl.BlockSpec((1, N_MOD - N_MOD_HEAD, d), lambda i, j: (i * tm // seq, 0, 0)),
            pl.BlockSpec((1, d), lambda i, j: (0, 0)),
            pl.BlockSpec((d, tf), lambda i, j: (0, j)),
            pl.BlockSpec((tf, d), lambda i, j: (j, 0)),
            pl.BlockSpec((1, d), lambda i, j: (0, 0)),
        ],
        operands=(x1, mod, g, w1, w2, g_final),
        out_specs=[pl.BlockSpec((tm, d), lambda i, j: (i, 0))],
        out_shape=[jax.ShapeDtypeStruct((t, d), F32)],
        scratch=[pltpu.VMEM((tm, d), BF16)],
        temporaries=[((tm // mid_blocks, tf), F32), ((tm // mid_blocks, tf), BF16),
                     ((tm // mid_blocks, d), F32)],
    )[0]


def kernel(x, c, w_ada, b_ada, g_norm_mix, w_in, w_conv, b_conv, w_rg_a, b_rg_a, w_rg_x, b_rg_x,
           lru_lambda, g_attn_out, g_lru_out, w_out, g_norm_mlp, w_mlp_in, w_mlp_out, g_norm_final):
    batch, seq, d = x.shape
    assert w_ada.shape[0] == 1, "single layer: the final norm is fused into the MLP kernel"
    xt = x.reshape(batch * seq, d)
    c32 = c.astype(F32)
    n_head = N_MOD_HEAD * d
    mod_head, w_in_bf16 = _ada(c32, w_ada[0], b_ada, w_in[0], n_head)
    proj = _inproj(xt, mod_head.reshape(batch, N_MOD_HEAD, d), g_norm_mix, w_in_bf16, seq)
    o_attn, w1, wo, mod_rest = _attention(proj, w_mlp_in[0], w_out[0], c32, w_ada[0], b_ada, n_head,
                                          batch, seq)
    mod_late = mod_rest.reshape(batch, N_MOD - N_MOD_HEAD, d)
    w_gates = jnp.concatenate([w_rg_a[0], w_rg_x[0]], axis=-1).astype(BF16)
    x1, w2 = _mix(proj, o_attn, xt, mod_late, w_conv[0], b_conv, w_gates, b_rg_a, b_rg_x, lru_lambda,
                  g_attn_out, g_lru_out, wo, w_mlp_out[0], seq)
    out = _mlp(x1, mod_late, g_norm_mlp, w1, w2, g_norm_final[None, :], seq)
    return out.reshape(batch, seq, d)
```

```python
import functools

import jax
import jax.numpy as jnp
from jax import lax
from jax.experimental import pallas as pl
from jax.experimental.pallas import tpu as pltpu

F32 = jnp.float32
BF16 = jnp.bfloat16

ATTN_HEADS = 8
HEAD_DIM = 128
LRU_HEADS = 8
LRU_BLOCK = 128
CONV_WIDTH = 4
LRU_C = 8.0
N_MOD = 6
EPS = 1e-6

LOG2_E = 1.4426950408889634
EXP2_UNDERFLOW = -151.0
MASKED_LOGIT = 1e30

SCORES_AHEAD = 2
STREAM_BUFFERS = 4

V7X_VMEM_BYTES = 64 * 2 ** 20
VMEM_LANES = 128


def _nbytes(shape, dtype):
    lanes = -(-shape[-1] // VMEM_LANES) * VMEM_LANES
    n = lanes * jnp.dtype(dtype).itemsize
    for s in shape[:-1]:
        n *= s
    return n


def _vmem_limit(specs, dtypes, scratch=(), temporaries=()):
    need = sum(2 * _nbytes(s.block_shape, dt) for s, dt in zip(specs, dtypes)
               if s.block_shape is not None)
    need += sum(_nbytes(s.shape, s.dtype) for s in scratch if s.memory_space == pltpu.VMEM)
    need += sum(_nbytes(shape, dt) for shape, dt in temporaries)
    assert need <= V7X_VMEM_BYTES, need
    return need


def _call(body, name, grid, in_specs, operands, out_specs, out_shape, scratch=(), temporaries=()):
    dtypes = [o.dtype for o in operands] + [o.dtype for o in out_shape]
    return pl.pallas_call(
        body, grid=grid, in_specs=in_specs, out_specs=out_specs, out_shape=out_shape,
        scratch_shapes=list(scratch),
        compiler_params=pltpu.CompilerParams(
            dimension_semantics=("arbitrary",) * len(grid),
            vmem_limit_bytes=_vmem_limit(list(in_specs) + list(out_specs), dtypes, scratch, temporaries)),
        name=name,
    )(*operands)


def _softplus(x):
    return jnp.maximum(x, 0.0) + jnp.log1p(jnp.exp(-jnp.abs(x)))


def _rms_scale(x):
    return lax.rsqrt(jnp.mean(x * x, axis=-1, keepdims=True) + EPS)


def _modulation(c_ref, w_ref, b_ref):
    c = c_ref[...]
    c_act = c * jax.nn.sigmoid(c)
    return jnp.dot(c_act.astype(BF16), w_ref[...].astype(BF16),
                   preferred_element_type=F32) + b_ref[...]


def _ada_kernel(c_ref, w_ref, b_ref, winf_ref, o_ref, winb_ref):
    winb_ref[...] = winf_ref[...].astype(BF16)
    o_ref[...] = _modulation(c_ref, w_ref, b_ref)


def _ada(c_pad, w_ada, b_ada, w_in, n, tn=512):
    m, d = c_pad.shape
    steps = n // tn
    win_chunk = pl.BlockSpec((w_in.shape[0] // steps, w_in.shape[1]), lambda j: (j, 0))
    return _call(
        _ada_kernel, "ada", (steps,),
        in_specs=[
            pl.BlockSpec((m, d), lambda j: (0, 0)),
            pl.BlockSpec((d, tn), lambda j: (0, j)),
            pl.BlockSpec((1, tn), lambda j: (0, j)),
            win_chunk,
        ],
        operands=(c_pad, w_ada, b_ada, w_in),
        out_specs=[pl.BlockSpec((m, tn), lambda j: (0, j)), win_chunk],
        out_shape=[jax.ShapeDtypeStruct((m, n), F32), jax.ShapeDtypeStruct(w_in.shape, BF16)],
        temporaries=[((d, tn), BF16)],
    )


def _adaln(x, g, shift, scale):
    return ((x * _rms_scale(x) * g) * (1.0 + scale) + shift).astype(BF16)


def _inproj_kernel(x_ref, mod_ref, g_ref, w_ref, cs_ref, o_ref, h_ref, *, row_blocks):
    j = pl.program_id(1)
    rb = x_ref.shape[0] // row_blocks

    def project(h):
        return (jnp.dot(h, w_ref[...], preferred_element_type=F32) * cs_ref[...]).astype(BF16)

    @pl.when(j == 0)
    def _():
        for r in range(row_blocks):
            rows = slice(r * rb, (r + 1) * rb)
            h = _adaln(x_ref[rows, :], g_ref[...], mod_ref[0, 0:1, :], mod_ref[0, 1:2, :])
            h_ref[rows, :] = h
            o_ref[rows, :] = project(h)

    @pl.when(j > 0)
    def _():
        o_ref[...] = project(h_ref[...])


def _inproj(x2, mod_mix, g, w_in, seq, tm=1024, tn=1280):
    t, d = x2.shape
    n = w_in.shape[1]
    nt, nj = t // tm, n // tn
    col = lax.broadcasted_iota(jnp.int32, (1, n), 1)
    colscale = jnp.where(col < ATTN_HEADS * HEAD_DIM, HEAD_DIM ** -0.5 * LOG2_E, 1.0).astype(F32)
    return _call(
        functools.partial(_inproj_kernel, row_blocks=4), "inproj", (nt, nj),
        in_specs=[
            pl.BlockSpec((tm, d), lambda i, j: (i, 0)),
            pl.BlockSpec((1, 2, d), lambda i, j: (i * tm // seq, 0, 0)),
            pl.BlockSpec((1, d), lambda i, j: (0, 0)),
            pl.BlockSpec((d, tn), lambda i, j: (0, j)),
            pl.BlockSpec((1, tn), lambda i, j: (0, j)),
        ],
        operands=(x2, mod_mix, g, w_in, colscale),
        out_specs=[pl.BlockSpec((tm, tn), lambda i, j: (i, j))],
        out_shape=[jax.ShapeDtypeStruct((t, n), BF16)],
        scratch=[pltpu.VMEM((tm, d), BF16)],
        temporaries=[((tm, tn), F32)],
    )[0]


def _attn_kernel(q_ref, k_ref, v_ref, u_ref, w1_hbm, wof_ref, cond_ref, wa_hbm, ba_ref,
                 o_ref, w1b_ref, wob_ref, modr_ref, c_ref, w1_buf, wa_buf, sems, *, tq, nq, n_done):
    step = pl.program_id(0) * pl.num_programs(1) + pl.program_id(1)
    nsteps = pl.num_programs(0) * pl.num_programs(1)
    cw, ca = w1_buf.shape[2], wa_buf.shape[2]

    def copies(chunk):
        slot = chunk % STREAM_BUFFERS
        return (pltpu.make_async_copy(w1_hbm.at[:, pl.ds(pl.multiple_of(chunk * cw, cw), cw)],
                                      w1_buf.at[slot], sems.at[0, slot]),
                pltpu.make_async_copy(wa_hbm.at[:, pl.ds(pl.multiple_of(n_done + chunk * ca, ca), ca)],
                                      wa_buf.at[slot], sems.at[1, slot]))

    @pl.when(step == 0)
    def _():
        for chunk in range(STREAM_BUFFERS - 1):
            for cp in copies(chunk):
                cp.start()

    @pl.when(step + (STREAM_BUFFERS - 1) < nsteps)
    def _():
        for cp in copies(step + (STREAM_BUFFERS - 1)):
            cp.start()

    for cp in copies(step):
        cp.wait()
    slot = step % STREAM_BUFFERS
    w1b_ref[...] = w1_buf[slot].astype(BF16)
    wob_ref[...] = wof_ref[...].astype(BF16)
    modr_ref[...] = _modulation(cond_ref, wa_buf.at[slot], ba_ref)

    u = u_ref[...]
    rows = lax.broadcasted_iota(jnp.int32, (tq, tq), 0)
    cols = lax.broadcasted_iota(jnp.int32, (tq, tq), 1)
    causal = cols < rows

    hps = q_ref.shape[1] // HEAD_DIM
    lanes = lambda hd: slice(hd * HEAD_DIM, (hd + 1) * HEAD_DIM)

    def scores(hd, qb, start, mask):
        k = k_ref[pl.ds(start, tq), lanes(hd)]
        z = lax.dot_general(qb, k, (((1,), (1,)), ((), ())), preferred_element_type=F32)
        if mask is not None:
            z = jnp.where(mask, z, -MASKED_LOGIT)
        t = jnp.log2(1.0 + jnp.exp2(-jnp.abs(z)))
        log_beta = jnp.minimum(z, 0.0) - t
        return log_beta, log_beta - z

    def weights(log_beta, log_stay, c):
        after = jnp.dot(log_stay.astype(BF16), u, preferred_element_type=F32)
        e = after + log_beta if c is None else after + (log_beta + c)
        return jnp.exp2(e).astype(BF16), after[:, 0:1] + log_stay[:, 0:1]

    pairs = [(hd, i, j) for i in range(nq) for hd in range(hps)
             for j in ((i, i - 1) if i > 0 else (i,))]
    carry = {}
    acc = {}

    def finish(hd, i, j, lb, ls):
        v = v_ref[j * tq:(j + 1) * tq, lanes(hd)]
        if j == i:
            w, carry[hd, i] = weights(lb, ls, None)
            acc[hd, i] = jnp.dot(w, v, preferred_element_type=F32)
        else:
            w, rs = weights(lb, ls, carry[hd, i])
            acc[hd, i] = acc[hd, i] + jnp.dot(w, v, preferred_element_type=F32)
            carry[hd, i] = carry[hd, i] + rs
        if j == max(i - 1, 0):
            o_ref[i * tq:(i + 1) * tq, lanes(hd)] = acc.pop((hd, i))
            c = carry.pop((hd, i))
            if i > 1:
                c_ref[hd, i] = c

    pending = []
    for hd, i, j in pairs:
        pending.append((hd, i, j) + scores(hd, q_ref[i * tq:(i + 1) * tq, lanes(hd)], j * tq,
                                           causal if j == i else None))
        if len(pending) > SCORES_AHEAD:
            finish(*pending.pop(0))
    for p in pending:
        finish(*p)

    if nq > 2:
        @pl.when(jnp.max(c_ref[:, 2:nq]) > EXP2_UNDERFLOW)
        def _():
            for hd, i in [(hd, i) for hd in range(hps) for i in range(2, nq)]:
                lo_row = i * tq
                qb = q_ref[lo_row:lo_row + tq, lanes(hd)]

                def cond(state):
                    j, alive, _ = state
                    return jnp.logical_and(j >= 0, alive)

                def body(state, hd=hd, lo_row=lo_row, qb=qb):
                    j, _, c = state
                    start = pl.multiple_of(j * tq, tq)
                    lb, ls = scores(hd, qb, start, None)
                    w, rs = weights(lb, ls, c)
                    o_ref[lo_row:lo_row + tq, lanes(hd)] += jnp.dot(
                        w, v_ref[pl.ds(start, tq), lanes(hd)], preferred_element_type=F32)
                    c = c + rs
                    return j - 1, jnp.max(c) > EXP2_UNDERFLOW, c

                c0 = c_ref[hd, i]
                lax.while_loop(cond, body, (i - 2, jnp.max(c0) > EXP2_UNDERFLOW, c0))


def _attention(proj, w_mlp_in, w_out, c_pad, w_ada, b_ada, n_done, batch, seq, tq=256, hps=2):
    t = proj.shape[0]
    nq = seq // tq
    r = lax.broadcasted_iota(jnp.int32, (tq, tq), 0)
    s = lax.broadcasted_iota(jnp.int32, (tq, tq), 1)
    u = (r > s).astype(BF16)
    h = ATTN_HEADS // hps
    steps = batch * h
    d, f = w_mlp_in.shape
    w1_chunk = pl.BlockSpec((d, f // steps), lambda b, hh: (0, b * h + hh))
    wo_chunk = pl.BlockSpec((w_out.shape[0] // steps, w_out.shape[1]), lambda b, hh: (b * h + hh, 0))
    head = lambda off: pl.BlockSpec((seq, hps * HEAD_DIM), lambda b, hh: (b, off + hh))
    m = c_pad.shape[0]
    n_rest = w_ada.shape[1] - n_done
    ta = n_rest // steps
    ada_chunk = lambda rows: pl.BlockSpec((rows, ta), lambda b, hh: (0, n_done // ta + b * h + hh))
    assert steps >= STREAM_BUFFERS - 1
    hbm = pl.BlockSpec(memory_space=pl.ANY)
    return _call(
        functools.partial(_attn_kernel, tq=tq, nq=nq, n_done=n_done), "attn", (batch, h),
        in_specs=[head(0), head(h), head(2 * h), pl.BlockSpec((tq, tq), lambda b, hh: (0, 0)),
                  hbm, wo_chunk, pl.BlockSpec((m, d), lambda b, hh: (0, 0)),
                  hbm, ada_chunk(1)],
        operands=(proj, proj, proj, u, w_mlp_in, w_out, c_pad, w_ada, b_ada),
        out_specs=[head(0), w1_chunk, wo_chunk,
                   pl.BlockSpec((m, ta), lambda b, hh: (0, b * h + hh))],
        out_shape=[jax.ShapeDtypeStruct((t, ATTN_HEADS * HEAD_DIM), F32),
                   jax.ShapeDtypeStruct(w_mlp_in.shape, BF16),
                   jax.ShapeDtypeStruct(w_out.shape, BF16),
                   jax.ShapeDtypeStruct((m, n_rest), F32)],
        scratch=[pltpu.VMEM((hps, nq, tq, 1), F32),
                 pltpu.VMEM((STREAM_BUFFERS, d, f // steps), F32),
                 pltpu.VMEM((STREAM_BUFFERS, d, ta), F32),
                 pltpu.SemaphoreType.DMA((2, STREAM_BUFFERS))],
        temporaries=[((tq, tq), F32)] * (3 * (SCORES_AHEAD + 2)),
    )


def _mix_kernel(xr_ref, xg_ref, oa_ref, x_ref, mod_ref, wc_ref, bc_ref, wg_ref, ba_ref, bx_ref,
                lam_ref, ga_ref, gl_ref, wo_ref, w2f_ref, o_ref, w2b_ref,
                ext_ref, hc_ref, ol_ref, nl_ref, *, ts, ns):
    g = pl.program_id(0)
    hist = 8
    half = oa_ref.shape[1]

    w2b_ref[...] = w2f_ref[...].astype(BF16)

    @pl.when(g % ns == 0)
    def _():
        ext_ref[0:hist, :] = jnp.zeros((hist, ext_ref.shape[1]), F32)
        hc_ref[...] = jnp.zeros_like(hc_ref)

    @pl.when(g == 0)
    def _():
        nl_ref[...] = jnp.zeros_like(nl_ref)

    oa = oa_ref[...]
    na = (oa * _rms_scale(oa) * ga_ref[...]).astype(BF16)
    mixed = jnp.concatenate([na, nl_ref[(g + 1) % 2]], axis=1)
    ncol = o_ref.shape[1] // LRU_HEADS

    def project(h):
        cs = slice(h * ncol, (h + 1) * ncol)
        y = jnp.dot(mixed, wo_ref[:, cs], preferred_element_type=F32)
        o_ref[:, cs] = x_ref[:, cs] + mod_ref[0, 2:3, cs] * y

    ext_ref[hist:hist + ts, :] = xr_ref[...].astype(F32)
    nv = ts // 8
    sub = lax.broadcasted_iota(jnp.int32, (nv, 8, LRU_BLOCK), 1)

    def lru_head(h):
        sl = slice(h * LRU_BLOCK, (h + 1) * LRU_BLOCK)
        wc = wc_ref[:, sl]
        xe = ext_ref[0:hist + ts, sl].reshape(nv + 1, 8, LRU_BLOCK)
        xc = bc_ref[:, sl] + wc[CONV_WIDTH - 1:CONV_WIDTH, :] * xe[1:]
        for d in range(1, CONV_WIDTH):
            rolled = pltpu.roll(xe, d, 1)
            tap = CONV_WIDTH - 1 - d
            xc = xc + wc[tap:tap + 1, :] * jnp.where(sub >= d, rolled[1:], rolled[:-1])
        xc = xc.reshape(ts, LRU_BLOCK)
        gates = jnp.dot(xc.astype(BF16), wg_ref[h], preferred_element_type=F32)
        r = jax.nn.sigmoid(gates[:, :LRU_BLOCK] + ba_ref[:, sl])
        gi = jax.nn.sigmoid(gates[:, LRU_BLOCK:] + bx_ref[:, sl])
        log_a = (-LRU_C) * r * _softplus(-lam_ref[:, sl])
        a = jnp.exp(log_a)
        one_minus_a2 = -jnp.tanh(log_a) * (1.0 + a * a)
        b = jnp.exp2(jnp.log(one_minus_a2) * (0.5 * LOG2_E)) * (gi * xc)
        a = a.reshape(nv, 8, LRU_BLOCK)
        b = b.reshape(nv, 8, LRU_BLOCK)
        for d in (1, 2, 4):
            keep = sub >= d
            a_prev = pltpu.roll(a, d, 1)
            b_prev = pltpu.roll(b, d, 1)
            b = b + a * jnp.where(keep, b_prev, 0.0)
            a = a * jnp.where(keep, a_prev, 1.0)
        gate = jax.nn.gelu(xg_ref[:, sl].astype(F32), approximate=True).reshape(nv, 8, LRU_BLOCK)
        carry = hc_ref[0:1, sl]
        for i in range(nv):
            hcur = b[i] + a[i] * carry
            ol_ref[8 * i:8 * i + 8, sl] = hcur * gate[i]
            carry = hcur[7:8, :]
        hc_ref[0:1, sl] = carry

    for h in range(LRU_HEADS):
        project(h)
        lru_head(h)

    ext_ref[0:hist, :] = ext_ref[ts:ts + hist, :]
    ol = ol_ref[...]
    nl_ref[g % 2] = (ol * _rms_scale(ol) * gl_ref[...]).astype(BF16)


def _mix(proj, o_attn, x2, mod, w_conv, b_conv, w_gates, b_a, b_x, lam, g_a, g_l, w_out, w_mlp_out,
         seq, ts=256):
    t, d = x2.shape
    c = LRU_HEADS * LRU_BLOCK
    ns = seq // ts
    nblk = t // ts
    cur = lambda g: jnp.minimum(g, nblk - 1)
    prev = lambda g: jnp.maximum(g - 1, 0)
    vec = pl.BlockSpec((1, c), lambda g: (0, 0))
    xr_col = 3 * ATTN_HEADS * HEAD_DIM // c
    f = w_mlp_out.shape[0]
    w2_chunk = pl.BlockSpec((f // nblk, d), lambda g: (cur(g), 0))
    return _call(
        functools.partial(_mix_kernel, ts=ts, ns=ns), "mix", (nblk + 1,),
        in_specs=[
            pl.BlockSpec((ts, c), lambda g: (cur(g), xr_col)),
            pl.BlockSpec((ts, c), lambda g: (cur(g), xr_col + 1)),
            pl.BlockSpec((ts, c), lambda g: (prev(g), 0)),
            pl.BlockSpec((ts, d), lambda g: (prev(g), 0)),
            pl.BlockSpec((1, N_MOD, d), lambda g: (prev(g) // ns, 0, 0)),
            pl.BlockSpec((CONV_WIDTH, c), lambda g: (0, 0)),
            vec,
            pl.BlockSpec((LRU_HEADS, LRU_BLOCK, 2 * LRU_BLOCK), lambda g: (0, 0, 0)),
            vec, vec, vec, vec, vec,
            pl.BlockSpec((2 * c, d), lambda g: (0, 0)),
            w2_chunk,
        ],
        operands=(proj, proj, o_attn, x2, mod, w_conv, b_conv, w_gates, b_a, b_x, lam, g_a, g_l,
                  w_out, w_mlp_out),
        out_specs=[pl.BlockSpec((ts, d), lambda g: (prev(g), 0)), w2_chunk],
        out_shape=[jax.ShapeDtypeStruct((t, d), F32), jax.ShapeDtypeStruct(w_mlp_out.shape, BF16)],
        scratch=[pltpu.VMEM((ts + 8, c), F32), pltpu.VMEM((8, c), F32),
                 pltpu.VMEM((ts, c), F32), pltpu.VMEM((2, ts, c), BF16)],
        temporaries=[((ts, 2 * c), BF16), ((ts, c), F32), ((ts, c), F32)],
    )


def _mlp_kernel(x_ref, mod_ref, g_ref, w1_ref, w2_ref, gf_ref, o_ref, h_ref, *, row_blocks,
                mid_blocks):
    j = pl.program_id(1)
    last = pl.num_programs(1) - 1
    rb = x_ref.shape[0] // row_blocks

    def expand(h):
        hid = jnp.dot(h, w1_ref[...], preferred_element_type=F32)
        hid = jnp.square(jnp.maximum(hid, 0.0)).astype(BF16)
        return jnp.dot(hid, w2_ref[...], preferred_element_type=F32)

    @pl.when(j == 0)
    def _():
        for r in range(row_blocks):
            rows = slice(r * rb, (r + 1) * rb)
            h = _adaln(x_ref[rows, :], g_ref[...], mod_ref[0, 3:4, :], mod_ref[0, 4:5, :])
            h_ref[rows, :] = h
            o_ref[rows, :] = expand(h)

    @pl.when(jnp.logical_and(j > 0, j < last))
    def _():
        mb = x_ref.shape[0] // mid_blocks
        for r in range(mid_blocks):
            rows = slice(r * mb, (r + 1) * mb)
            o_ref[rows, :] += expand(h_ref[rows, :])

    @pl.when(j == last)
    def _():
        for r in range(row_blocks):
            rows = slice(r * rb, (r + 1) * rb)
            x2 = x_ref[rows, :] + mod_ref[0, 5:6, :] * (o_ref[rows, :] + expand(h_ref[rows, :]))
            o_ref[rows, :] = x2 * _rms_scale(x2) * gf_ref[...]


def _mlp(x1, mod, g, w1, w2, g_final, seq, tm=512, tf=1024):
    t, d = x1.shape
    f = w1.shape[1]
    assert f // tf >= 2, "first and last hidden-chunk steps are distinct code paths"
    row_blocks, mid_blocks = 2, 1
    return _call(
        functools.partial(_mlp_kernel, row_blocks=row_blocks, mid_blocks=mid_blocks), "mlp",
        (t // tm, f // tf),
        in_specs=[
            pl.BlockSpec((tm, d), lambda i, j: (i, 0)),
            pl.BlockSpec((1, N_MOD, d), lambda i, j: (i * tm // seq, 0, 0)),
            pl.BlockSpec((1, d), lambda i, j: (0, 0)),
            pl.BlockSpec((d, tf), lambda i, j: (0, j)),
            pl.BlockSpec((tf, d), lambda i, j: (j, 0)),
            pl.BlockSpec((1, d), lambda i, j: (0, 0)),
        ],
        operands=(x1, mod, g, w1, w2, g_final),
        out_specs=[pl.BlockSpec((tm, d), lambda i, j: (i, 0))],
        out_shape=[jax.ShapeDtypeStruct((t, d), F32)],
        scratch=[pltpu.VMEM((tm, d), BF16)],
        temporaries=[((tm // mid_blocks, tf), F32), ((tm // mid_blocks, tf), BF16),
                     ((tm // mid_blocks, d), F32)],
    )[0]


def kernel(x, c, w_ada, b_ada, g_norm_mix, w_in, w_conv, b_conv, w_rg_a, b_rg_a, w_rg_x, b_rg_x,
           lru_lambda, g_attn_out, g_lru_out, w_out, g_norm_mlp, w_mlp_in, w_mlp_out, g_norm_final):
    batch, seq, d = x.shape
    assert w_ada.shape[0] == 1, "single layer: the final norm is fused into the MLP kernel"
    xt = x.reshape(batch * seq, d)
    c_pad = jnp.pad(c.astype(F32), ((0, 8 - batch), (0, 0)))
    mod_head, w_in_bf16 = _ada(c_pad, w_ada[0], b_ada, w_in[0], 2 * d)
    proj = _inproj(xt, mod_head[:batch].reshape(batch, 2, d), g_norm_mix, w_in_bf16, seq)
    o_attn, w1, wo, mod_rest = _attention(proj, w_mlp_in[0], w_out[0], c_pad, w_ada[0], b_ada, 2 * d,
                                          batch, seq)
    mod = jnp.concatenate([mod_head, mod_rest], axis=1)[:batch].reshape(batch, N_MOD, d)
    w_gates = jnp.concatenate([w_rg_a[0], w_rg_x[0]], axis=-1).astype(BF16)
    x1, w2 = _mix(proj, o_attn, xt, mod, w_conv[0], b_conv, w_gates, b_rg_a, b_rg_x, lru_lambda,
                  g_attn_out, g_lru_out, wo, w_mlp_out[0], seq)
    out = _mlp(x1, mod, g_norm_mlp, w1, w2, g_norm_final[None, :], seq)
    return out.reshape(batch, seq, d)
```

```python
import functools

import jax
import jax.numpy as jnp
from jax import lax
from jax.experimental import pallas as pl
from jax.experimental.pallas import tpu as pltpu

F32 = jnp.float32
BF16 = jnp.bfloat16

ATTN_HEADS = 8
HEAD_DIM = 128
LRU_HEADS = 8
LRU_BLOCK = 128
CONV_WIDTH = 4
LRU_C = 8.0
N_MOD = 6
EPS = 1e-6

LOG2_E = 1.4426950408889634
EXP2_UNDERFLOW = -151.0
MASKED_LOGIT = 1e30

SCORES_AHEAD = 2
STREAM_BUFFERS = 3

V7X_VMEM_BYTES = 64 * 2 ** 20
VMEM_LANES = 128


def _nbytes(shape, dtype):
    lanes = -(-shape[-1] // VMEM_LANES) * VMEM_LANES
    n = lanes * jnp.dtype(dtype).itemsize
    for s in shape[:-1]:
        n *= s
    return n


def _vmem_limit(specs, dtypes, scratch=(), temporaries=()):
    need = sum(2 * _nbytes(s.block_shape, dt) for s, dt in zip(specs, dtypes)
               if s.block_shape is not None)
    need += sum(_nbytes(s.shape, s.dtype) for s in scratch if s.memory_space == pltpu.VMEM)
    need += sum(_nbytes(shape, dt) for shape, dt in temporaries)
    assert need <= V7X_VMEM_BYTES, need
    return need


def _call(body, name, grid, in_specs, operands, out_specs, out_shape, scratch=(), temporaries=()):
    dtypes = [o.dtype for o in operands] + [o.dtype for o in out_shape]
    return pl.pallas_call(
        body, grid=grid, in_specs=in_specs, out_specs=out_specs, out_shape=out_shape,
        scratch_shapes=list(scratch),
        compiler_params=pltpu.CompilerParams(
            dimension_semantics=("arbitrary",) * len(grid),
            vmem_limit_bytes=_vmem_limit(list(in_specs) + list(out_specs), dtypes, scratch, temporaries)),
        name=name,
    )(*operands)


def _softplus(x):
    return jnp.maximum(x, 0.0) + jnp.log1p(jnp.exp(-jnp.abs(x)))


def _rms_scale(x):
    return lax.rsqrt(jnp.mean(x * x, axis=-1, keepdims=True) + EPS)


def _modulation(c_ref, w_ref, b_ref):
    c = c_ref[...]
    c_act = c * jax.nn.sigmoid(c)
    return jnp.dot(c_act.astype(BF16), w_ref[...].astype(BF16),
                   preferred_element_type=F32) + b_ref[...]


def _ada_kernel(c_ref, w_ref, b_ref, winf_ref, o_ref, winb_ref):
    winb_ref[...] = winf_ref[...].astype(BF16)
    o_ref[...] = _modulation(c_ref, w_ref, b_ref)


def _ada(c_pad, w_ada, b_ada, w_in, n, tn=512):
    m, d = c_pad.shape
    steps = n // tn
    win_chunk = pl.BlockSpec((w_in.shape[0] // steps, w_in.shape[1]), lambda j: (j, 0))
    return _call(
        _ada_kernel, "ada", (steps,),
        in_specs=[
            pl.BlockSpec((m, d), lambda j: (0, 0)),
            pl.BlockSpec((d, tn), lambda j: (0, j)),
            pl.BlockSpec((1, tn), lambda j: (0, j)),
            win_chunk,
        ],
        operands=(c_pad, w_ada, b_ada, w_in),
        out_specs=[pl.BlockSpec((m, tn), lambda j: (0, j)), win_chunk],
        out_shape=[jax.ShapeDtypeStruct((m, n), F32), jax.ShapeDtypeStruct(w_in.shape, BF16)],
        temporaries=[((d, tn), BF16)],
    )


def _adaln(x, g, shift, scale):
    return ((x * _rms_scale(x) * g) * (1.0 + scale) + shift).astype(BF16)


def _inproj_kernel(x_ref, mod_ref, g_ref, w_ref, cs_ref, o_ref, h_ref, *, row_blocks):
    j = pl.program_id(1)
    rb = x_ref.shape[0] // row_blocks

    def project(h):
        return (jnp.dot(h, w_ref[...], preferred_element_type=F32) * cs_ref[...]).astype(BF16)

    @pl.when(j == 0)
    def _():
        for r in range(row_blocks):
            rows = slice(r * rb, (r + 1) * rb)
            h = _adaln(x_ref[rows, :], g_ref[...], mod_ref[0, 0:1, :], mod_ref[0, 1:2, :])
            h_ref[rows, :] = h
            o_ref[rows, :] = project(h)

    @pl.when(j > 0)
    def _():
        o_ref[...] = project(h_ref[...])


def _inproj(x2, mod_mix, g, w_in, seq, tm=1024, tn=1280):
    t, d = x2.shape
    n = w_in.shape[1]
    nt, nj = t // tm, n // tn
    col = lax.broadcasted_iota(jnp.int32, (1, n), 1)
    colscale = jnp.where(col < ATTN_HEADS * HEAD_DIM, HEAD_DIM ** -0.5 * LOG2_E, 1.0).astype(F32)
    return _call(
        functools.partial(_inproj_kernel, row_blocks=4), "inproj", (nt, nj),
        in_specs=[
            pl.BlockSpec((tm, d), lambda i, j: (i, 0)),
            pl.BlockSpec((1, 2, d), lambda i, j: (i * tm // seq, 0, 0)),
            pl.BlockSpec((1, d), lambda i, j: (0, 0)),
            pl.BlockSpec((d, tn), lambda i, j: (0, j)),
            pl.BlockSpec((1, tn), lambda i, j: (0, j)),
        ],
        operands=(x2, mod_mix, g, w_in, colscale),
        out_specs=[pl.BlockSpec((tm, tn), lambda i, j: (i, j))],
        out_shape=[jax.ShapeDtypeStruct((t, n), BF16)],
        scratch=[pltpu.VMEM((tm, d), BF16)],
        temporaries=[((tm, tn), F32)],
    )[0]


def _attn_kernel(q_ref, k_ref, v_ref, u_ref, w1_hbm, wof_ref, cond_ref, wa_hbm, ba_ref,
                 o_ref, w1b_ref, wob_ref, modr_ref, c_ref, w1_buf, wa_buf, sems, *, tq, nq, n_done):
    step = pl.program_id(0) * pl.num_programs(1) + pl.program_id(1)
    nsteps = pl.num_programs(0) * pl.num_programs(1)
    cw, ca = w1_buf.shape[2], wa_buf.shape[2]

    def copies(chunk):
        slot = chunk % STREAM_BUFFERS
        return (pltpu.make_async_copy(w1_hbm.at[:, pl.ds(pl.multiple_of(chunk * cw, cw), cw)],
                                      w1_buf.at[slot], sems.at[0, slot]),
                pltpu.make_async_copy(wa_hbm.at[:, pl.ds(pl.multiple_of(n_done + chunk * ca, ca), ca)],
                                      wa_buf.at[slot], sems.at[1, slot]))

    @pl.when(step == 0)
    def _():
        for chunk in range(STREAM_BUFFERS - 1):
            for cp in copies(chunk):
                cp.start()

    @pl.when(step + (STREAM_BUFFERS - 1) < nsteps)
    def _():
        for cp in copies(step + (STREAM_BUFFERS - 1)):
            cp.start()

    for cp in copies(step):
        cp.wait()
    slot = step % STREAM_BUFFERS
    w1b_ref[...] = w1_buf[slot].astype(BF16)
    wob_ref[...] = wof_ref[...].astype(BF16)
    modr_ref[...] = _modulation(cond_ref, wa_buf.at[slot], ba_ref)

    u = u_ref[...]
    rows = lax.broadcasted_iota(jnp.int32, (tq, tq), 0)
    cols = lax.broadcasted_iota(jnp.int32, (tq, tq), 1)
    causal = cols < rows

    hps = q_ref.shape[1] // HEAD_DIM
    lanes = lambda hd: slice(hd * HEAD_DIM, (hd + 1) * HEAD_DIM)

    def scores(hd, qb, start, mask):
        k = k_ref[pl.ds(start, tq), lanes(hd)]
        z = lax.dot_general(qb, k, (((1,), (1,)), ((), ())), preferred_element_type=F32)
        if mask is not None:
            z = jnp.where(mask, z, -MASKED_LOGIT)
        t = jnp.log2(1.0 + jnp.exp2(-jnp.abs(z)))
        log_beta = jnp.minimum(z, 0.0) - t
        return log_beta, log_beta - z

    def weights(log_beta, log_stay, c):
        after = jnp.dot(log_stay.astype(BF16), u, preferred_element_type=F32)
        e = after + log_beta if c is None else after + (log_beta + c)
        return jnp.exp2(e).astype(BF16), after[:, 0:1] + log_stay[:, 0:1]

    pairs = [(hd, i, j) for i in range(nq) for hd in range(hps)
             for j in ((i, i - 1) if i > 0 else (i,))]
    carry = {}
    acc = {}

    def finish(hd, i, j, lb, ls):
        v = v_ref[j * tq:(j + 1) * tq, lanes(hd)]
        if j == i:
            w, carry[hd, i] = weights(lb, ls, None)
            acc[hd, i] = jnp.dot(w, v, preferred_element_type=F32)
        else:
            w, rs = weights(lb, ls, carry[hd, i])
            acc[hd, i] = acc[hd, i] + jnp.dot(w, v, preferred_element_type=F32)
            carry[hd, i] = carry[hd, i] + rs
        if j == max(i - 1, 0):
            o_ref[i * tq:(i + 1) * tq, lanes(hd)] = acc.pop((hd, i))
            c = carry.pop((hd, i))
            if i > 1:
                c_ref[hd, i] = c

    pending = []
    for hd, i, j in pairs:
        pending.append((hd, i, j) + scores(hd, q_ref[i * tq:(i + 1) * tq, lanes(hd)], j * tq,
                                           causal if j == i else None))
        if len(pending) > SCORES_AHEAD:
            finish(*pending.pop(0))
    for p in pending:
        finish(*p)

    if nq > 2:
        @pl.when(jnp.max(c_ref[:, 2:nq]) > EXP2_UNDERFLOW)
        def _():
            for hd, i in [(hd, i) for hd in range(hps) for i in range(2, nq)]:
                lo_row = i * tq
                qb = q_ref[lo_row:lo_row + tq, lanes(hd)]

                def cond(state):
                    j, alive, _ = state
                    return jnp.logical_and(j >= 0, alive)

                def body(state, hd=hd, lo_row=lo_row, qb=qb):
                    j, _, c = state
                    start = pl.multiple_of(j * tq, tq)
                    lb, ls = scores(hd, qb, start, None)
                    w, rs = weights(lb, ls, c)
                    o_ref[lo_row:lo_row + tq, lanes(hd)] += jnp.dot(
                        w, v_ref[pl.ds(start, tq), lanes(hd)], preferred_element_type=F32)
                    c = c + rs
                    return j - 1, jnp.max(c) > EXP2_UNDERFLOW, c

                c0 = c_ref[hd, i]
                lax.while_loop(cond, body, (i - 2, jnp.max(c0) > EXP2_UNDERFLOW, c0))


def _attention(proj, w_mlp_in, w_out, c_pad, w_ada, b_ada, n_done, batch, seq, tq=256, hps=2):
    t = proj.shape[0]
    nq = seq // tq
    r = lax.broadcasted_iota(jnp.int32, (tq, tq), 0)
    s = lax.broadcasted_iota(jnp.int32, (tq, tq), 1)
    u = (r > s).astype(BF16)
    h = ATTN_HEADS // hps
    steps = batch * h
    d, f = w_mlp_in.shape
    w1_chunk = pl.BlockSpec((d, f // steps), lambda b, hh: (0, b * h + hh))
    wo_chunk = pl.BlockSpec((w_out.shape[0] // steps, w_out.shape[1]), lambda b, hh: (b * h + hh, 0))
    head = lambda off: pl.BlockSpec((seq, hps * HEAD_DIM), lambda b, hh: (b, off + hh))
    m = c_pad.shape[0]
    n_rest = w_ada.shape[1] - n_done
    ta = n_rest // steps
    ada_chunk = lambda rows: pl.BlockSpec((rows, ta), lambda b, hh: (0, n_done // ta + b * h + hh))
    assert steps >= STREAM_BUFFERS - 1
    hbm = pl.BlockSpec(memory_space=pl.ANY)
    return _call(
        functools.partial(_attn_kernel, tq=tq, nq=nq, n_done=n_done), "attn", (batch, h),
        in_specs=[head(0), head(h), head(2 * h), pl.BlockSpec((tq, tq), lambda b, hh: (0, 0)),
                  hbm, wo_chunk, pl.BlockSpec((m, d), lambda b, hh: (0, 0)),
                  hbm, ada_chunk(1)],
        operands=(proj, proj, proj, u, w_mlp_in, w_out, c_pad, w_ada, b_ada),
        out_specs=[head(0), w1_chunk, wo_chunk,
                   pl.BlockSpec((m, ta), lambda b, hh: (0, b * h + hh))],
        out_shape=[jax.ShapeDtypeStruct((t, ATTN_HEADS * HEAD_DIM), F32),
                   jax.ShapeDtypeStruct(w_mlp_in.shape, BF16),
                   jax.ShapeDtypeStruct(w_out.shape, BF16),
                   jax.ShapeDtypeStruct((m, n_rest), F32)],
        scratch=[pltpu.VMEM((hps, nq, tq, 1), F32),
                 pltpu.VMEM((STREAM_BUFFERS, d, f // steps), F32),
                 pltpu.VMEM((STREAM_BUFFERS, d, ta), F32),
                 pltpu.SemaphoreType.DMA((2, STREAM_BUFFERS))],
        temporaries=[((tq, tq), F32)] * (3 * (SCORES_AHEAD + 2)),
    )


def _mix_kernel(xr_ref, xg_ref, oa_ref, x_ref, mod_ref, wc_ref, bc_ref, wg_ref, ba_ref, bx_ref,
                lam_ref, ga_ref, gl_ref, wo_ref, w2_hbm, o_ref, w2b_ref,
                ext_ref, hc_ref, ol_ref, nl_ref, w2_buf, sems, *, ts, ns):
    g = pl.program_id(0)
    hist = 8
    half = oa_ref.shape[1]

    nchunk = pl.num_programs(0) - 1
    rows = w2_buf.shape[1]

    def copy(chunk):
        return pltpu.make_async_copy(
            w2_hbm.at[pl.ds(pl.multiple_of(chunk * rows, rows), rows), :],
            w2_buf.at[chunk % STREAM_BUFFERS], sems.at[chunk % STREAM_BUFFERS])

    @pl.when(g == 0)
    def _():
        for chunk in range(STREAM_BUFFERS - 1):
            copy(chunk).start()

    @pl.when(g + (STREAM_BUFFERS - 1) < nchunk)
    def _():
        copy(g + (STREAM_BUFFERS - 1)).start()

    @pl.when(g < nchunk)
    def _():
        copy(g).wait()

    w2b_ref[...] = w2_buf[jnp.minimum(g, nchunk - 1) % STREAM_BUFFERS].astype(BF16)

    @pl.when(g % ns == 0)
    def _():
        ext_ref[0:hist, :] = jnp.zeros((hist, ext_ref.shape[1]), F32)
        hc_ref[...] = jnp.zeros_like(hc_ref)

    @pl.when(g == 0)
    def _():
        nl_ref[...] = jnp.zeros_like(nl_ref)

    oa = oa_ref[...]
    na = (oa * _rms_scale(oa) * ga_ref[...]).astype(BF16)
    mixed = jnp.concatenate([na, nl_ref[(g + 1) % 2]], axis=1)
    ncol = o_ref.shape[1] // LRU_HEADS

    def project(h):
        cs = slice(h * ncol, (h + 1) * ncol)
        y = jnp.dot(mixed, wo_ref[:, cs], preferred_element_type=F32)
        o_ref[:, cs] = x_ref[:, cs] + mod_ref[0, 2:3, cs] * y

    ext_ref[hist:hist + ts, :] = xr_ref[...].astype(F32)
    nv = ts // 8
    sub = lax.broadcasted_iota(jnp.int32, (nv, 8, LRU_BLOCK), 1)

    def lru_head(h):
        sl = slice(h * LRU_BLOCK, (h + 1) * LRU_BLOCK)
        wc = wc_ref[:, sl]
        xe = ext_ref[0:hist + ts, sl].reshape(nv + 1, 8, LRU_BLOCK)
        xc = bc_ref[:, sl] + wc[CONV_WIDTH - 1:CONV_WIDTH, :] * xe[1:]
        for d in range(1, CONV_WIDTH):
            rolled = pltpu.roll(xe, d, 1)
            tap = CONV_WIDTH - 1 - d
            xc = xc + wc[tap:tap + 1, :] * jnp.where(sub >= d, rolled[1:], rolled[:-1])
        xc = xc.reshape(ts, LRU_BLOCK)
        gates = jnp.dot(xc.astype(BF16), wg_ref[h], preferred_element_type=F32)
        r = jax.nn.sigmoid(gates[:, :LRU_BLOCK] + ba_ref[:, sl])
        gi = jax.nn.sigmoid(gates[:, LRU_BLOCK:] + bx_ref[:, sl])
        log_a = (-LRU_C) * r * _softplus(-lam_ref[:, sl])
        a = jnp.exp(log_a)
        one_minus_a2 = -jnp.tanh(log_a) * (1.0 + a * a)
        b = jnp.exp2(jnp.log(one_minus_a2) * (0.5 * LOG2_E)) * (gi * xc)
        a = a.reshape(nv, 8, LRU_BLOCK)
        b = b.reshape(nv, 8, LRU_BLOCK)
        for d in (1, 2, 4):
            keep = sub >= d
            a_prev = pltpu.roll(a, d, 1)
            b_prev = pltpu.roll(b, d, 1)
            b = b + a * jnp.where(keep, b_prev, 0.0)
            a = a * jnp.where(keep, a_prev, 1.0)
        gate = jax.nn.gelu(xg_ref[:, sl].astype(F32), approximate=True).reshape(nv, 8, LRU_BLOCK)
        carry = hc_ref[0:1, sl]
        for i in range(nv):
            hcur = b[i] + a[i] * carry
            ol_ref[8 * i:8 * i + 8, sl] = hcur * gate[i]
            carry = hcur[7:8, :]
        hc_ref[0:1, sl] = carry

    for h in range(LRU_HEADS):
        project(h)
        lru_head(h)

    ext_ref[0:hist, :] = ext_ref[ts:ts + hist, :]
    ol = ol_ref[...]
    nl_ref[g % 2] = (ol * _rms_scale(ol) * gl_ref[...]).astype(BF16)


def _mix(proj, o_attn, x2, mod, w_conv, b_conv, w_gates, b_a, b_x, lam, g_a, g_l, w_out, w_mlp_out,
         seq, ts=256):
    t, d = x2.shape
    c = LRU_HEADS * LRU_BLOCK
    ns = seq // ts
    nblk = t // ts
    cur = lambda g: jnp.minimum(g, nblk - 1)
    prev = lambda g: jnp.maximum(g - 1, 0)
    vec = pl.BlockSpec((1, c), lambda g: (0, 0))
    xr_col = 3 * ATTN_HEADS * HEAD_DIM // c
    f = w_mlp_out.shape[0]
    w2_chunk = pl.BlockSpec((f // nblk, d), lambda g: (cur(g), 0))
    return _call(
        functools.partial(_mix_kernel, ts=ts, ns=ns), "mix", (nblk + 1,),
        in_specs=[
            pl.BlockSpec((ts, c), lambda g: (cur(g), xr_col)),
            pl.BlockSpec((ts, c), lambda g: (cur(g), xr_col + 1)),
            pl.BlockSpec((ts, c), lambda g: (prev(g), 0)),
            pl.BlockSpec((ts, d), lambda g: (prev(g), 0)),
            pl.BlockSpec((1, N_MOD, d), lambda g: (prev(g) // ns, 0, 0)),
            pl.BlockSpec((CONV_WIDTH, c), lambda g: (0, 0)),
            vec,
            pl.BlockSpec((LRU_HEADS, LRU_BLOCK, 2 * LRU_BLOCK), lambda g: (0, 0, 0)),
            vec, vec, vec, vec, vec,
            pl.BlockSpec((2 * c, d), lambda g: (0, 0)),
            pl.BlockSpec(memory_space=pl.ANY),
        ],
        operands=(proj, proj, o_attn, x2, mod, w_conv, b_conv, w_gates, b_a, b_x, lam, g_a, g_l,
                  w_out, w_mlp_out),
        out_specs=[pl.BlockSpec((ts, d), lambda g: (prev(g), 0)), w2_chunk],
        out_shape=[jax.ShapeDtypeStruct((t, d), F32), jax.ShapeDtypeStruct(w_mlp_out.shape, BF16)],
        scratch=[pltpu.VMEM((ts + 8, c), F32), pltpu.VMEM((8, c), F32),
                 pltpu.VMEM((ts, c), F32), pltpu.VMEM((2, ts, c), BF16),
                 pltpu.VMEM((STREAM_BUFFERS, f // nblk, d), F32),
                 pltpu.SemaphoreType.DMA((STREAM_BUFFERS,))],
        temporaries=[((ts, 2 * c), BF16), ((ts, c), F32), ((ts, c), F32)],
    )


def _mlp_kernel(x_ref, mod_ref, g_ref, w1_ref, w2_ref, gf_ref, o_ref, h_ref, *, row_blocks,
                mid_blocks):
    j = pl.program_id(1)
    last = pl.num_programs(1) - 1
    rb = x_ref.shape[0] // row_blocks

    def expand(h):
        hid = jnp.dot(h, w1_ref[...], preferred_element_type=F32)
        hid = jnp.square(jnp.maximum(hid, 0.0)).astype(BF16)
        return jnp.dot(hid, w2_ref[...], preferred_element_type=F32)

    @pl.when(j == 0)
    def _():
        for r in range(row_blocks):
            rows = slice(r * rb, (r + 1) * rb)
            h = _adaln(x_ref[rows, :], g_ref[...], mod_ref[0, 3:4, :], mod_ref[0, 4:5, :])
            h_ref[rows, :] = h
            o_ref[rows, :] = expand(h)

    @pl.when(jnp.logical_and(j > 0, j < last))
    def _():
        mb = x_ref.shape[0] // mid_blocks
        for r in range(mid_blocks):
            rows = slice(r * mb, (r + 1) * mb)
            o_ref[rows, :] += expand(h_ref[rows, :])

    @pl.when(j == last)
    def _():
        for r in range(row_blocks):
            rows = slice(r * rb, (r + 1) * rb)
            x2 = x_ref[rows, :] + mod_ref[0, 5:6, :] * (o_ref[rows, :] + expand(h_ref[rows, :]))
            o_ref[rows, :] = x2 * _rms_scale(x2) * gf_ref[...]


def _mlp(x1, mod, g, w1, w2, g_final, seq, tm=512, tf=1024):
    t, d = x1.shape
    f = w1.shape[1]
    assert f // tf >= 2, "first and last hidden-chunk steps are distinct code paths"
    row_blocks, mid_blocks = 2, 1
    return _call(
        functools.partial(_mlp_kernel, row_blocks=row_blocks, mid_blocks=mid_blocks), "mlp",
        (t // tm, f // tf),
        in_specs=[
            pl.BlockSpec((tm, d), lambda i, j: (i, 0)),
            pl.BlockSpec((1, N_MOD, d), lambda i, j: (i * tm // seq, 0, 0)),
            pl.BlockSpec((1, d), lambda i, j: (0, 0)),
            pl.BlockSpec((d, tf), lambda i, j: (0, j)),
            pl.BlockSpec((tf, d), lambda i, j: (j, 0)),
            pl.BlockSpec((1, d), lambda i, j: (0, 0)),
        ],
        operands=(x1, mod, g, w1, w2, g_final),
        out_specs=[pl.BlockSpec((tm, d), lambda i, j: (i, 0))],
        out_shape=[jax.ShapeDtypeStruct((t, d), F32)],
        scratch=[pltpu.VMEM((tm, d), BF16)],
        temporaries=[((tm // mid_blocks, tf), F32), ((tm // mid_blocks, tf), BF16),
                     ((tm // mid_blocks, d), F32)],
    )[0]


def kernel(x, c, w_ada, b_ada, g_norm_mix, w_in, w_conv, b_conv, w_rg_a, b_rg_a, w_rg_x, b_rg_x,
           lru_lambda, g_attn_out, g_lru_out, w_out, g_norm_mlp, w_mlp_in, w_mlp_out, g_norm_final):
    batch, seq, d = x.shape
    assert w_ada.shape[0] == 1, "single layer: the final norm is fused into the MLP kernel"
    xt = x.reshape(batch * seq, d)
    c_pad = jnp.pad(c.astype(F32), ((0, 8 - batch), (0, 0)))
    mod_head, w_in_bf16 = _ada(c_pad, w_ada[0], b_ada, w_in[0], 2 * d)
    proj = _inproj(xt, mod_head[:batch].reshape(batch, 2, d), g_norm_mix, w_in_bf16, seq)
    o_attn, w1, wo, mod_rest = _attention(proj, w_mlp_in[0], w_out[0], c_pad, w_ada[0], b_ada, 2 * d,
                                          batch, seq)
    mod = jnp.concatenate([mod_head, mod_rest], axis=1)[:batch].reshape(batch, N_MOD, d)
    w_gates = jnp.concatenate([w_rg_a[0], w_rg_x[0]], axis=-1).astype(BF16)
    x1, w2 = _mix(proj, o_attn, xt, mod, w_conv[0], b_conv, w_gates, b_rg_a, b_rg_x, lru_lambda,
                  g_attn_out, g_lru_out, wo, w_mlp_out[0], seq)
    out = _mlp(x1, mod, g_norm_mlp, w1, w2, g_norm_final[None, :], seq)
    return out.reshape(batch, seq, d)
```
